```python
import math
import jax, jax.numpy as jnp
from jax import lax
import numpy as np

D_MODEL = 1024
BATCH = 4
SEQ = 8192
DEPTH = 4

N_MIXERS = 2
N_HEADS = 4
QK_DIM = D_MODEL // (2 * N_HEADS)
V_DIM = D_MODEL // N_HEADS
CHUNK = 128
GATE_CAP = 15.0
CONV_WIDTH = 3
D_FF = 4 * D_MODEL
EPS = 1e-6
N_MLSTM_LAYERS = (DEPTH + 1) // 2
N_CONV_LAYERS = DEPTH // 2
QK_COLS = N_HEADS * QK_DIM
MLSTM_IN = 2 * QK_COLS + 2 * D_MODEL + 2 * N_HEADS
CONV_IN = 3 * D_MODEL

kernel_name = "hybrid_mlstm_shortconv_sqrelu"


def rms_norm(x, g):
    xf = x.astype(jnp.float32)
    y = xf * lax.rsqrt(jnp.mean(xf * xf, axis=-1, keepdims=True) + EPS)
    return (y * g.astype(jnp.float32)).astype(x.dtype)


def mlstm_mixer(x, w_in, b_gates, g_hnorm, w_out):
    B_, S, _ = x.shape
    nc = S // CHUNK
    proj = x @ w_in
    q, k, v, o, gates = jnp.split(
        proj, [QK_COLS, 2 * QK_COLS, 2 * QK_COLS + D_MODEL, 2 * QK_COLS + 2 * D_MODEL], axis=-1)
    f32 = jnp.float32
    q = q.astype(f32).reshape(B_, S, N_HEADS, QK_DIM).transpose(0, 2, 1, 3)
    k = (k.astype(f32) * (QK_DIM ** -0.5)).reshape(B_, S, N_HEADS, QK_DIM).transpose(0, 2, 1, 3)
    v = v.astype(f32).reshape(B_, S, N_HEADS, V_DIM).transpose(0, 2, 1, 3)
    gates = gates.astype(f32) + b_gates.astype(f32)
    gates = GATE_CAP * jnp.tanh(gates / GATE_CAP)
    ig = gates[..., :N_HEADS].transpose(0, 2, 1)
    logf = jax.nn.log_sigmoid(gates[..., N_HEADS:]).transpose(0, 2, 1)

    qc = q.reshape(B_, N_HEADS, nc, CHUNK, QK_DIM)
    kc = k.reshape(B_, N_HEADS, nc, CHUNK, QK_DIM)
    vc = v.reshape(B_, N_HEADS, nc, CHUNK, V_DIM)
    igc = ig.reshape(B_, N_HEADS, nc, CHUNK)
    bcum = jnp.cumsum(logf.reshape(B_, N_HEADS, nc, CHUNK), axis=-1)
    b_last = bcum[..., -1]

    log_w = b_last[..., None] - bcum + igc
    m_loc = jnp.max(log_w, axis=-1)
    w = jnp.exp(log_w - m_loc[..., None])
    kv_loc = jnp.einsum('bhcl,bhcld,bhcle->bhcde', w, kc, vc)
    n_loc = jnp.einsum('bhcl,bhcld->bhcd', w, kc)

    def step(carry, inp):
        C, n, m = carry
        bl, ml, kvl, nl = inp
        m_new = jnp.maximum(bl + m, ml)
        a = jnp.exp(bl + m - m_new)
        s = jnp.exp(ml - m_new)
        C_new = a[..., None, None] * C + s[..., None, None] * kvl
        n_new = a[..., None] * n + s[..., None] * nl
        return (C_new, n_new, m_new), (C, n, m)

    init = (jnp.zeros((B_, N_HEADS, QK_DIM, V_DIM), f32),
            jnp.zeros((B_, N_HEADS, QK_DIM), f32),
            jnp.zeros((B_, N_HEADS), f32))
    xs = (jnp.moveaxis(b_last, 2, 0), jnp.moveaxis(m_loc, 2, 0),
          jnp.moveaxis(kv_loc, 2, 0), jnp.moveaxis(n_loc, 2, 0))
    _, (C_prev, n_prev, m_prev) = lax.scan(step, init, xs)
    C_prev = jnp.moveaxis(C_prev, 0, 2)
    n_prev = jnp.moveaxis(n_prev, 0, 2)
    m_prev = jnp.moveaxis(m_prev, 0, 2)

    causal = jnp.tril(jnp.ones((CHUNK, CHUNK), dtype=bool))
    logD = bcum[..., :, None] - bcum[..., None, :] + igc[..., None, :]
    logD = jnp.where(causal, logD, -jnp.inf)
    log_inter = bcum + m_prev[..., None]
    m_t = jnp.maximum(jnp.max(logD, axis=-1), log_inter)
    Dm = jnp.exp(logD - m_t[..., None])
    scores = jnp.einsum('bhcld,bhcsd->bhcls', qc, kc) * Dm
    inter = jnp.exp(log_inter - m_t)
    num = (jnp.einsum('bhcls,bhcse->bhcle', scores, vc)
           + inter[..., None] * jnp.einsum('bhcld,bhcde->bhcle', qc, C_prev))
    den = jnp.sum(scores, axis=-1) + inter * jnp.einsum('bhcld,bhcd->bhcl', qc, n_prev)
    h = num / jnp.maximum(jnp.abs(den), jnp.exp(-m_t))[..., None]

    h = h.reshape(B_, N_HEADS, S, V_DIM).transpose(0, 2, 1, 3)
    h = h * lax.rsqrt(jnp.mean(h * h, axis=-1, keepdims=True) + EPS)
    h = h * g_hnorm.astype(f32).reshape(N_HEADS, V_DIM)
    h = jax.nn.sigmoid(o.astype(f32)) * h.reshape(B_, S, D_MODEL)
    return h.astype(x.dtype) @ w_out


def short_conv_mixer(x, w_in, w_conv, w_out):
    proj = x @ w_in
    bgate, cgate, xh = jnp.split(proj, 3, axis=-1)
    u = cgate * xh
    conv = lax.conv_general_dilated(
        u, w_conv[:, None, :].astype(u.dtype), window_strides=(1,),
        padding=[(CONV_WIDTH - 1, 0)],
        dimension_numbers=('NWC', 'WIO', 'NWC'),
        feature_group_count=D_MODEL)
    return (bgate * conv) @ w_out


def sqrelu_mlp(x, w_up, w_down):
    h = jax.nn.relu(x @ w_up)
    return (h * h) @ w_down


def setup_inputs(seed: int = 0) -> dict:
    key = jax.random.key(seed)
    ks = jax.random.split(key, 12)
    f32 = jnp.float32
    x = jax.random.normal(ks[0], (BATCH, SEQ, D_MODEL), f32)
    norm_g = 1.0 + 0.05 * jax.random.normal(ks[1], (DEPTH, 4, D_MODEL), f32)
    w_in_mlstm = jax.random.normal(ks[2], (N_MLSTM_LAYERS, D_MODEL, MLSTM_IN), f32) * D_MODEL ** -0.5
    i_bias = -3.0 + 0.1 * jax.random.normal(ks[3], (N_MLSTM_LAYERS, N_HEADS), f32)
    f_bias = (jnp.linspace(3.0, 6.0, N_HEADS, dtype=f32)[None, :]
              + 0.1 * jax.random.normal(ks[4], (N_MLSTM_LAYERS, N_HEADS), f32))
    b_gates_mlstm = jnp.concatenate([i_bias, f_bias], axis=-1)
    g_hnorm = 1.0 + 0.05 * jax.random.normal(ks[5], (N_MLSTM_LAYERS, D_MODEL), f32)
    w_out_mlstm = jax.random.normal(ks[6], (N_MLSTM_LAYERS, D_MODEL, D_MODEL), f32) * D_MODEL ** -0.5
    w_in_conv = jax.random.normal(ks[7], (N_CONV_LAYERS, D_MODEL, CONV_IN), f32) * D_MODEL ** -0.5
    w_conv = jax.random.normal(ks[8], (N_CONV_LAYERS, CONV_WIDTH, D_MODEL), f32) * CONV_WIDTH ** -0.5
    w_out_conv = jax.random.normal(ks[9], (N_CONV_LAYERS, D_MODEL, D_MODEL), f32) * D_MODEL ** -0.5
    w_mlp_up = jax.random.normal(ks[10], (DEPTH, D_MODEL, D_FF), f32) * D_MODEL ** -0.5
    w_mlp_down = jax.random.normal(ks[11], (DEPTH, D_FF, D_MODEL), f32) * D_FF ** -0.5
    return {"x": x, "norm_g": norm_g, "w_in_mlstm": w_in_mlstm, "b_gates_mlstm": b_gates_mlstm,
            "g_hnorm": g_hnorm, "w_out_mlstm": w_out_mlstm, "w_in_conv": w_in_conv,
            "w_conv": w_conv, "w_out_conv": w_out_conv, "w_mlp_up": w_mlp_up,
            "w_mlp_down": w_mlp_down}


def reference(x, norm_g, w_in_mlstm, b_gates_mlstm, g_hnorm, w_out_mlstm, w_in_conv,
              w_conv, w_out_conv, w_mlp_up, w_mlp_down):
    for i in range(DEPTH):
        g = norm_g[i]
        j = i // N_MIXERS
        h = rms_norm(x, g[0])
        if i % N_MIXERS == 0:
            h = mlstm_mixer(h, w_in_mlstm[j], b_gates_mlstm[j], g_hnorm[j], w_out_mlstm[j])
        else:
            h = short_conv_mixer(h, w_in_conv[j], w_conv[j], w_out_conv[j])
        x = x + rms_norm(h, g[1])
        h = sqrelu_mlp(rms_norm(x, g[2]), w_mlp_up[i], w_mlp_down[i])
        x = x + rms_norm(h, g[3])
    return x
```

```python
import functools

import jax
import jax.numpy as jnp
from jax import lax
from jax.experimental import pallas as pl
from jax.experimental.pallas import tpu as pltpu

D_MODEL = 1024
DEPTH = 4
N_HEADS = 4
QK_DIM = 128
V_DIM = 256
CHUNK = 128
GATE_CAP = 15.0
D_FF = 4 * D_MODEL
EPS = 1e-6
QK_COLS = N_HEADS * QK_DIM

LANES = 128
AUG = V_DIM + LANES

TM_MLSTM = 512
TM_CONV = 512
TM_MLP = 512
FF_CHUNK = 1024
CONV_HALO = 8
VMEM_LIMIT = 56 * 1024 * 1024

F32 = jnp.float32
BF16 = jnp.bfloat16


def _rms(x, g):
    y = x * lax.rsqrt(jnp.mean(x * x, axis=-1, keepdims=True) + EPS)
    return y * g


def _dot(a, b):
    return jnp.dot(a, b, preferred_element_type=F32)


def _scan_rows(x, op, identity):
    n = x.shape[0]
    rows = lax.broadcasted_iota(jnp.int32, x.shape, 0)
    k = 1
    while k < n:
        shifted = pltpu.roll(x, k, axis=0)
        x = op(x, jnp.where(rows >= k, shifted, identity))
        k *= 2
    return x


def _mlp_kernel(x_ref, g_ref, wup_ref, wdn_ref, o_ref):
    x = x_ref[...]
    h = _rms(x, g_ref[0:1, :]).astype(BF16)
    acc = None
    for c in range(D_FF // FF_CHUNK):
        lo, hi = c * FF_CHUNK, (c + 1) * FF_CHUNK
        u = jnp.maximum(_dot(h, wup_ref[:, lo:hi]), 0.0)
        d = _dot((u * u).astype(BF16), wdn_ref[lo:hi, :])
        acc = d if acc is None else acc + d
    o_ref[...] = x + _rms(acc, g_ref[1:2, :])


def _mlp_layer(x2, g, w_up, w_down):
    n = x2.shape[0]
    const = lambda i: (0, 0)
    return pl.pallas_call(
        _mlp_kernel,
        out_shape=jax.ShapeDtypeStruct(x2.shape, x2.dtype),
        grid=(n // TM_MLP,),
        in_specs=[
            pl.BlockSpec((TM_MLP, D_MODEL), lambda i: (i, 0)),
            pl.BlockSpec((2, D_MODEL), const),
            pl.BlockSpec((D_MODEL, D_FF), const, pipeline_mode=pl.Buffered(1)),
            pl.BlockSpec((D_FF, D_MODEL), const, pipeline_mode=pl.Buffered(1)),
        ],
        out_specs=pl.BlockSpec((TM_MLP, D_MODEL), lambda i: (i, 0)),
        compiler_params=pltpu.CompilerParams(
            dimension_semantics=("arbitrary",), vmem_limit_bytes=VMEM_LIMIT),
        name="mlp",
    )(x2, g, w_up.astype(BF16), w_down.astype(BF16))


def _conv_kernel(x_ref, g_ref, wb_ref, wc_ref, wx_ref, wcv_ref, wout_ref, o_ref, u_s):
    tm = x_ref.shape[1]

    @pl.when(pl.program_id(1) == 0)
    def _():
        u_s[0:CONV_HALO, :] = jnp.zeros((CONV_HALO, D_MODEL), F32)

    x = x_ref[0]
    hn = _rms(x, g_ref[0:1, :]).astype(BF16)
    u_s[CONV_HALO:, :] = _dot(hn, wc_ref[...]) * _dot(hn, wx_ref[...])
    ua = u_s[...]
    u1 = pltpu.roll(ua, 1, axis=0)[CONV_HALO:]
    u2 = pltpu.roll(ua, 2, axis=0)[CONV_HALO:]
    conv = wcv_ref[0:1, :] * u2 + wcv_ref[1:2, :] * u1 + wcv_ref[2:3, :] * ua[CONV_HALO:]
    u_s[0:CONV_HALO, :] = ua[tm:, :]
    bg = _dot(hn, wb_ref[...])
    y = _dot((bg * conv).astype(BF16), wout_ref[...])
    o_ref[0] = x + _rms(y, g_ref[1:2, :])


def _conv_layer(x, g, w_in, w_conv, w_out):
    b, s, d = x.shape
    tm = TM_CONV
    const = lambda i, j: (0, 0)
    w_in = w_in.astype(BF16)
    return pl.pallas_call(
        _conv_kernel,
        out_shape=jax.ShapeDtypeStruct(x.shape, x.dtype),
        grid=(b, s // tm),
        in_specs=[
            pl.BlockSpec((1, tm, d), lambda i, j: (i, j, 0)),
            pl.BlockSpec((2, d), const),
            pl.BlockSpec((d, d), const),
            pl.BlockSpec((d, d), const),
            pl.BlockSpec((d, d), const),
            pl.BlockSpec((3, d), const),
            pl.BlockSpec((d, d), const),
        ],
        out_specs=pl.BlockSpec((1, tm, d), lambda i, j: (i, j, 0)),
        scratch_shapes=[pltpu.VMEM((tm + CONV_HALO, d), F32)],
        compiler_params=pltpu.CompilerParams(
            dimension_semantics=("arbitrary", "arbitrary"), vmem_limit_bytes=VMEM_LIMIT),
        name="conv_mixer",
    )(x, g, w_in[:, :d], w_in[:, d:2 * d], w_in[:, 2 * d:], w_conv, w_out.astype(BF16))


def _mlstm_kernel(x_ref, g_ref, wq_ref, wkt_ref, wv_ref, wo_ref, wg_ref, bg_ref, ghn_ref, wout_ref,
                  o_ref, hn_s, q_s, kt_s, v_s, gi_s, lf_s, h_s, c_s, m_s):
    tm = x_ref.shape[1]
    n_chunks = tm // CHUNK

    @pl.when(pl.program_id(1) == 0)
    def _():
        c_s[...] = jnp.zeros(c_s.shape, F32)
        m_s[...] = jnp.zeros(m_s.shape, F32)

    x = x_ref[0]
    hn = _rms(x, g_ref[0:1, :]).astype(BF16)
    hn_s[...] = hn
    q_s[...] = _dot(hn, wq_ref[...]).astype(BF16)
    kt = lax.dot_general(wkt_ref[...], hn, (((1,), (1,)), ((), ())), preferred_element_type=F32)
    kt = kt * (QK_DIM ** -0.5)
    for c in range(n_chunks):
        kt_s[c] = kt[:, c * CHUNK:(c + 1) * CHUNK].astype(BF16)
    v_s[...] = _dot(hn, wv_ref[...]).astype(BF16)
    gates = _dot(hn, wg_ref[...]) + bg_ref[...]
    gates = GATE_CAP * jnp.tanh(gates / GATE_CAP)
    gi_s[...] = gates[:, :LANES]
    gf = gates[:, LANES:]
    lf_s[...] = -(jnp.maximum(-gf, 0.0) + jnp.log1p(jnp.exp(-jnp.abs(gf))))

    causal = (lax.broadcasted_iota(jnp.int32, (CHUNK, CHUNK), 0)
              >= lax.broadcasted_iota(jnp.int32, (CHUNK, CHUNK), 1))
    ones = jnp.ones((CHUNK, LANES), BF16)

    def chunk_step(c, carry):
        r0 = pl.multiple_of(c * CHUNK, CHUNK)
        rows = pl.ds(r0, CHUNK)
        bcum = _scan_rows(lf_s[rows, :], jnp.add, 0.0)
        g = gi_s[rows, :] - bcum
        cmax = _scan_rows(g, jnp.maximum, -jnp.inf)
        m_prev = m_s[...]
        b_last = bcum[CHUNK - 1:CHUNK, :]
        m_row = jnp.maximum(cmax, m_prev)
        m_new = jnp.maximum(b_last + m_prev, b_last + cmax[CHUNK - 1:CHUNK, :])
        decay = jnp.exp(b_last + m_prev - m_new)
        w_in_state = jnp.exp(g + (b_last - m_new))
        inter = jnp.exp(m_prev - m_row)
        floor = jnp.exp(-(bcum + m_row))
        g_t = g.T
        w_t = w_in_state.T
        for h in range(N_HEADS):
            qh = q_s[rows, h * QK_DIM:(h + 1) * QK_DIM]
            kth = kt_s[c, h * QK_DIM:(h + 1) * QK_DIM, :]
            vaug = jnp.concatenate([v_s[rows, h * V_DIM:(h + 1) * V_DIM], ones], axis=1)
            e = jnp.where(causal, g_t[h:h + 1, :] - m_row[:, h:h + 1], -jnp.inf)
            scores = _dot(qh, kth) * jnp.exp(e)
            c_old = c_s[h]
            r = (_dot(scores.astype(BF16), vaug)
                 + inter[:, h:h + 1] * _dot(qh, c_old.astype(BF16)))
            den = jnp.maximum(jnp.abs(r[:, V_DIM:]), floor[:, h:h + 1])
            hh = r[:, :V_DIM] / jnp.concatenate([den, den], axis=1)
            hh = hh * lax.rsqrt(jnp.mean(hh * hh, axis=-1, keepdims=True) + EPS)
            h_s[rows, h * V_DIM:(h + 1) * V_DIM] = hh * ghn_ref[:, h * V_DIM:(h + 1) * V_DIM]
            kw = (kth.astype(F32) * w_t[h:h + 1, :]).astype(BF16)
            c_s[h] = decay[:, h:h + 1] * c_old + _dot(kw, vaug)
        m_s[...] = m_new
        return carry

    lax.fori_loop(0, n_chunks, chunk_step, 0)

    o = _dot(hn_s[...], wo_ref[...])
    y = _dot((jax.nn.sigmoid(o) * h_s[...]).astype(BF16), wout_ref[...])
    o_ref[0] = x + _rms(y, g_ref[1:2, :])


def _mlstm_layer(x, g, w_in, b_gates, g_hnorm, w_out):
    b, s, d = x.shape
    tm = TM_MLSTM
    n_chunks = tm // CHUNK
    const = lambda i, j: (0, 0)
    gate_cols = w_in[:, 2 * QK_COLS + 2 * d:]
    wg = jnp.zeros((d, 2 * LANES), F32)
    wg = wg.at[:, :N_HEADS].set(gate_cols[:, :N_HEADS]).at[:, LANES:LANES + N_HEADS].set(gate_cols[:, N_HEADS:])
    bg = jnp.zeros((1, 2 * LANES), F32)
    bg = bg.at[0, :N_HEADS].set(b_gates[:N_HEADS]).at[0, LANES:LANES + N_HEADS].set(b_gates[N_HEADS:])
    w_bf = w_in.astype(BF16)
    return pl.pallas_call(
        _mlstm_kernel,
        out_shape=jax.ShapeDtypeStruct(x.shape, x.dtype),
        grid=(b, s // tm),
        in_specs=[
            pl.BlockSpec((1, tm, d), lambda i, j: (i, j, 0)),
            pl.BlockSpec((2, d), const),
            pl.BlockSpec((d, QK_COLS), const),
            pl.BlockSpec((QK_COLS, d), const),
            pl.BlockSpec((d, d), const),
            pl.BlockSpec((d, d), const),
            pl.BlockSpec((d, 2 * LANES), const),
            pl.BlockSpec((1, 2 * LANES), const),
            pl.BlockSpec((1, d), const),
            pl.BlockSpec((d, d), const),
        ],
        out_specs=pl.BlockSpec((1, tm, d), lambda i, j: (i, j, 0)),
        scratch_shapes=[
            pltpu.VMEM((tm, d), BF16),
            pltpu.VMEM((tm, QK_COLS), BF16),
            pltpu.VMEM((n_chunks, QK_COLS, CHUNK), BF16),
            pltpu.VMEM((tm, d), BF16),
            pltpu.VMEM((tm, LANES), F32),
            pltpu.VMEM((tm, LANES), F32),
            pltpu.VMEM((tm, d), F32),
            pltpu.VMEM((N_HEADS, QK_DIM, AUG), F32),
            pltpu.VMEM((1, LANES), F32),
        ],
        compiler_params=pltpu.CompilerParams(
            dimension_semantics=("arbitrary", "arbitrary"), vmem_limit_bytes=VMEM_LIMIT),
        name="mlstm_mixer",
    )(x, g, w_bf[:, :QK_COLS], w_bf[:, QK_COLS:2 * QK_COLS].T, w_bf[:, 2 * QK_COLS:2 * QK_COLS + d],
      w_bf[:, 2 * QK_COLS + d:2 * QK_COLS + 2 * d], wg.astype(BF16), bg, g_hnorm.reshape(1, d),
      w_out.astype(BF16))


def kernel(x, norm_g, w_in_mlstm, b_gates_mlstm, g_hnorm, w_out_mlstm, w_in_conv, w_conv, w_out_conv,
           w_mlp_up, w_mlp_down):
    b, s, d = x.shape
    for i in range(DEPTH):
        g = norm_g[i]
        j = i // 2
        if i % 2 == 0:
            x = _mlstm_layer(x, g[0:2], w_in_mlstm[j], b_gates_mlstm[j], g_hnorm[j], w_out_mlstm[j])
        else:
            x = _conv_layer(x, g[0:2], w_in_conv[j], w_conv[j], w_out_conv[j])
        x = _mlp_layer(x.reshape(b * s, d), g[2:4], w_mlp_up[i], w_mlp_down[i]).reshape(b, s, d)
    return x
```

```python
import jax
import jax.numpy as jnp
from jax import lax
from jax.experimental import pallas as pl
from jax.experimental.pallas import tpu as pltpu

D_MODEL = 1024
DEPTH = 4
N_HEADS = 4
QK_DIM = 128
V_DIM = 256
CHUNK = 128
GATE_CAP = 15.0
D_FF = 4 * D_MODEL
EPS = 1e-6
QK_COLS = N_HEADS * QK_DIM

LANES = 128
SUBLANES = 8
AUG = V_DIM + LANES

TM_MLSTM = 512
TM_CONV = 512
TM_MLP = 512
FF_CHUNK = 1024
CONV_HALO = SUBLANES
VMEM_LIMIT = 56 * 1024 * 1024

F32 = jnp.float32
BF16 = jnp.bfloat16


def _rms(x, g):
    y = x * lax.rsqrt(jnp.mean(x * x, axis=-1, keepdims=True) + EPS)
    return y * g


def _dot(a, b):
    return jnp.dot(a, b, preferred_element_type=F32)


def _scan_rows(x, op, identity):
    n = x.shape[0]
    rows = lax.broadcasted_iota(jnp.int32, x.shape, 0)
    k = 1
    while k < n:
        shifted = pltpu.roll(x, k, axis=0)
        x = op(x, jnp.where(rows >= k, shifted, identity))
        k *= 2
    return x


def _mlp_kernel(x_ref, g_ref, wup_ref, wdn_ref, o_ref):
    x = x_ref[...]
    h = _rms(x, g_ref[0:1, :]).astype(BF16)
    acc = None
    for c in range(D_FF // FF_CHUNK):
        lo, hi = c * FF_CHUNK, (c + 1) * FF_CHUNK
        u = jnp.maximum(_dot(h, wup_ref[:, lo:hi]), 0.0)
        d = _dot((u * u).astype(BF16), wdn_ref[lo:hi, :])
        acc = d if acc is None else acc + d
    o_ref[...] = x + _rms(acc, g_ref[1:2, :])


def _mlp_layer(x2, g, w_up, w_down):
    n = x2.shape[0]
    const = lambda i: (0, 0)
    return pl.pallas_call(
        _mlp_kernel,
        out_shape=jax.ShapeDtypeStruct(x2.shape, x2.dtype),
        grid=(n // TM_MLP,),
        in_specs=[
            pl.BlockSpec((TM_MLP, D_MODEL), lambda i: (i, 0)),
            pl.BlockSpec((2, D_MODEL), const),
            pl.BlockSpec((D_MODEL, D_FF), const, pipeline_mode=pl.Buffered(1)),
            pl.BlockSpec((D_FF, D_MODEL), const, pipeline_mode=pl.Buffered(1)),
        ],
        out_specs=pl.BlockSpec((TM_MLP, D_MODEL), lambda i: (i, 0)),
        compiler_params=pltpu.CompilerParams(
            dimension_semantics=("arbitrary",), vmem_limit_bytes=VMEM_LIMIT),
        name="mlp",
    )(x2, g, w_up.astype(BF16), w_down.astype(BF16))


def _conv_kernel(x_ref, g_ref, wb_ref, wc_ref, wx_ref, wcv_ref, wout_ref, o_ref, u_s):
    tm = x_ref.shape[1]

    @pl.when(pl.program_id(1) == 0)
    def _():
        u_s[0:CONV_HALO, :] = jnp.zeros((CONV_HALO, D_MODEL), F32)

    x = x_ref[0]
    hn = _rms(x, g_ref[0:1, :]).astype(BF16)
    u_s[CONV_HALO:, :] = _dot(hn, wc_ref[...]) * _dot(hn, wx_ref[...])
    ua = u_s[...]
    u1 = pltpu.roll(ua, 1, axis=0)[CONV_HALO:]
    u2 = pltpu.roll(ua, 2, axis=0)[CONV_HALO:]
    conv = wcv_ref[0:1, :] * u2 + wcv_ref[1:2, :] * u1 + wcv_ref[2:3, :] * ua[CONV_HALO:]
    u_s[0:CONV_HALO, :] = ua[tm:, :]
    bg = _dot(hn, wb_ref[...])
    y = _dot((bg * conv).astype(BF16), wout_ref[...])
    o_ref[0] = x + _rms(y, g_ref[1:2, :])


def _conv_layer(x, g, w_in, w_conv, w_out):
    b, s, d = x.shape
    tm = TM_CONV
    const = lambda i, j: (0, 0)
    w_in = w_in.astype(BF16)
    return pl.pallas_call(
        _conv_kernel,
        out_shape=jax.ShapeDtypeStruct(x.shape, x.dtype),
        grid=(b, s // tm),
        in_specs=[
            pl.BlockSpec((1, tm, d), lambda i, j: (i, j, 0)),
            pl.BlockSpec((2, d), const),
            pl.BlockSpec((d, d), const),
            pl.BlockSpec((d, d), const),
            pl.BlockSpec((d, d), const),
            pl.BlockSpec((3, d), const),
            pl.BlockSpec((d, d), const),
        ],
        out_specs=pl.BlockSpec((1, tm, d), lambda i, j: (i, j, 0)),
        scratch_shapes=[pltpu.VMEM((tm + CONV_HALO, d), F32)],
        compiler_params=pltpu.CompilerParams(
            dimension_semantics=("arbitrary", "arbitrary"), vmem_limit_bytes=VMEM_LIMIT),
        name="conv_mixer",
    )(x, g, w_in[:, :d], w_in[:, d:2 * d], w_in[:, 2 * d:], w_conv, w_out.astype(BF16))


def _mlstm_kernel(x_ref, g_ref, wq_ref, wkt_ref, wv_ref, wo_ref, wg_ref, bg_ref, ghn_ref, wout_ref,
                  o_ref, v_s, h_s, c_s, m_s):
    tm = x_ref.shape[1]
    n_chunks = tm // CHUNK

    @pl.when(pl.program_id(1) == 0)
    def _():
        c_s[...] = jnp.zeros(c_s.shape, F32)
        m_s[...] = jnp.zeros(m_s.shape, F32)
        for h in range(N_HEADS):
            v_s[:, h * AUG + V_DIM:(h + 1) * AUG] = jnp.ones((tm, LANES), BF16)

    x = x_ref[0]
    hn = _rms(x, g_ref[0:1, :]).astype(BF16)

    gates = bg_ref[...]
    for c in range(n_chunks):
        gates = gates + _dot(hn[c * CHUNK:(c + 1) * CHUNK, :], wg_ref[c])
    gates = GATE_CAP * jnp.tanh(gates / GATE_CAP)
    gf = gates[:, LANES:]
    logf = -(jnp.maximum(-gf, 0.0) + jnp.log1p(jnp.exp(-jnp.abs(gf))))
    bcum = _scan_rows(logf, jnp.add, 0.0)
    g = gates[:, :LANES] - bcum
    cmax = _scan_rows(g, jnp.maximum, -jnp.inf)
    b_last = jnp.broadcast_to(bcum[CHUNK - 1:CHUNK, :], (SUBLANES, LANES))
    m_loc = b_last + jnp.broadcast_to(cmax[CHUNK - 1:CHUNK, :], (SUBLANES, LANES))

    lane = lax.broadcasted_iota(jnp.int32, (SUBLANES, LANES), 1)
    m_run = m_s[...]
    m_prev = jnp.zeros((SUBLANES, LANES), F32)
    for c in range(n_chunks):
        here = (lane >= N_HEADS * c) & (lane < N_HEADS * (c + 1))
        m_prev = jnp.where(here, m_run, m_prev)
        m_run = pltpu.roll(jnp.maximum(b_last + m_run, m_loc), N_HEADS, axis=1)
    m_s[...] = pltpu.roll(m_run, (LANES - N_HEADS * n_chunks) % LANES, axis=1)
    m_new = jnp.maximum(b_last + m_prev, m_loc)

    m_row = jnp.maximum(cmax, m_prev[0:1, :])
    decay = jnp.exp(b_last + m_prev - m_new)[0:1, :]
    inter = jnp.exp(m_prev[0:1, :] - m_row)
    floor = jnp.exp(-(bcum + m_row))
    g_t = g.T
    w_t = jnp.exp(g + (b_last - m_new)[0:1, :]).T

    q = _dot(hn, wq_ref[...])
    kt = lax.dot_general(wkt_ref[...], hn, (((1,), (1,)), ((), ())), preferred_element_type=F32)
    kt = kt * (QK_DIM ** -0.5)
    v = _dot(hn, wv_ref[...]).astype(BF16)
    for h in range(N_HEADS):
        v_s[:, h * AUG:h * AUG + V_DIM] = v[:, h * V_DIM:(h + 1) * V_DIM]

    causal = (lax.broadcasted_iota(jnp.int32, (CHUNK, CHUNK), 0)
              >= lax.broadcasted_iota(jnp.int32, (CHUNK, CHUNK), 1))
    pairs = [(c, h) for c in range(n_chunks) for h in range(N_HEADS)]
    rows_of = lambda c: slice(c * CHUNK, (c + 1) * CHUNK)
    vaug_of = lambda c, h: v_s[rows_of(c), h * AUG:(h + 1) * AUG]
    lhs, kv = {}, {}
    for c, h in pairs:
        r = N_HEADS * c + h
        qh = q[rows_of(c), h * QK_DIM:(h + 1) * QK_DIM]
        kth = kt[h * QK_DIM:(h + 1) * QK_DIM, rows_of(c)]
        e = jnp.where(causal, g_t[r:r + 1, :] - m_row[:, r:r + 1], -jnp.inf)
        scores = _dot(qh.astype(BF16), kth.astype(BF16)) * jnp.exp(e)
        lhs[c, h] = jnp.concatenate(
            [scores.astype(BF16), (qh * inter[:, r:r + 1]).astype(BF16)], axis=1)
        kv[c, h] = _dot((kth * w_t[r:r + 1, :]).astype(BF16), vaug_of(c, h))
    state = [c_s[h] for h in range(N_HEADS)]
    nd = {}
    for c, h in pairs:
        r = N_HEADS * c + h
        rhs = jnp.concatenate([vaug_of(c, h), state[h].astype(BF16)], axis=0)
        nd[c, h] = _dot(lhs[c, h], rhs)
        state[h] = decay[:, r:r + 1] * state[h] + kv[c, h]
    for h in range(N_HEADS):
        c_s[h] = state[h]
    for c, h in pairs:
        r = N_HEADS * c + h
        den = jnp.maximum(jnp.abs(nd[c, h][:, V_DIM:]), floor[:, r:r + 1])
        hh = nd[c, h][:, :V_DIM] / jnp.concatenate([den, den], axis=1)
        hh = hh * lax.rsqrt(jnp.mean(hh * hh, axis=-1, keepdims=True) + EPS)
        h_s[rows_of(c), h * V_DIM:(h + 1) * V_DIM] = hh * ghn_ref[:, h * V_DIM:(h + 1) * V_DIM]

    o = _dot(hn, wo_ref[...])
    y = _dot((jax.nn.sigmoid(o) * h_s[...]).astype(BF16), wout_ref[...])
    o_ref[0] = x + _rms(y, g_ref[1:2, :])


def _mlstm_layer(x, g, w_in, b_gates, g_hnorm, w_out):
    b, s, d = x.shape
    tm = TM_MLSTM
    n_chunks = tm // CHUNK
    assert N_HEADS * n_chunks <= LANES
    const = lambda i, j: (0, 0)
    gate_cols = w_in[:, 2 * QK_COLS + 2 * d:]
    wg = jnp.zeros((n_chunks, d, 2 * LANES), F32)
    bg = jnp.zeros((1, 2 * LANES), F32)
    for c in range(n_chunks):
        lo = N_HEADS * c
        wg = wg.at[c, :, lo:lo + N_HEADS].set(gate_cols[:, :N_HEADS])
        wg = wg.at[c, :, LANES + lo:LANES + lo + N_HEADS].set(gate_cols[:, N_HEADS:])
        bg = bg.at[0, lo:lo + N_HEADS].set(b_gates[:N_HEADS])
        bg = bg.at[0, LANES + lo:LANES + lo + N_HEADS].set(b_gates[N_HEADS:])
    w_bf = w_in.astype(BF16)
    return pl.pallas_call(
        _mlstm_kernel,
        out_shape=jax.ShapeDtypeStruct(x.shape, x.dtype),
        grid=(b, s // tm),
        in_specs=[
            pl.BlockSpec((1, tm, d), lambda i, j: (i, j, 0)),
            pl.BlockSpec((2, d), const),
            pl.BlockSpec((d, QK_COLS), const),
            pl.BlockSpec((QK_COLS, d), const),
            pl.BlockSpec((d, d), const),
            pl.BlockSpec((d, d), const),
            pl.BlockSpec((n_chunks, d, 2 * LANES), lambda i, j: (0, 0, 0)),
            pl.BlockSpec((1, 2 * LANES), const),
            pl.BlockSpec((1, d), const),
            pl.BlockSpec((d, d), const),
        ],
        out_specs=pl.BlockSpec((1, tm, d), lambda i, j: (i, j, 0)),
        scratch_shapes=[
            pltpu.VMEM((tm, N_HEADS * AUG), BF16),
            pltpu.VMEM((tm, d), F32),
            pltpu.VMEM((N_HEADS, QK_DIM, AUG), F32),
            pltpu.VMEM((SUBLANES, LANES), F32),
        ],
        compiler_params=pltpu.CompilerParams(
            dimension_semantics=("arbitrary", "arbitrary"), vmem_limit_bytes=VMEM_LIMIT),
        name="mlstm_mixer",
    )(x, g, w_bf[:, :QK_COLS], w_bf[:, QK_COLS:2 * QK_COLS].T, w_bf[:, 2 * QK_COLS:2 * QK_COLS + d],
      w_bf[:, 2 * QK_COLS + d:2 * QK_COLS + 2 * d], wg.astype(BF16), bg, g_hnorm.reshape(1, d),
      w_out.astype(BF16))


def kernel(x, norm_g, w_in_mlstm, b_gates_mlstm, g_hnorm, w_out_mlstm, w_in_conv, w_conv, w_out_conv,
           w_mlp_up, w_mlp_down):
    b, s, d = x.shape
    for i in range(DEPTH):
        g = norm_g[i]
        j = i // 2
        if i % 2 == 0:
            x = _mlstm_layer(x, g[0:2], w_in_mlstm[j], b_gates_mlstm[j], g_hnorm[j], w_out_mlstm[j])
        else:
            x = _conv_layer(x, g[0:2], w_in_conv[j], w_conv[j], w_out_conv[j])
        x = _mlp_layer(x.reshape(b * s, d), g[2:4], w_mlp_up[i], w_mlp_down[i]).reshape(b, s, d)
    return x
```

```python
import jax
import jax.numpy as jnp
from jax import lax
from jax.experimental import pallas as pl
from jax.experimental.pallas import tpu as pltpu

D_MODEL = 1024
DEPTH = 4
N_HEADS = 4
QK_DIM = 128
V_DIM = 256
CHUNK = 128
GATE_CAP = 15.0
D_FF = 4 * D_MODEL
EPS = 1e-6
QK_COLS = N_HEADS * QK_DIM

LANES = 128
SUBLANES = 8
AUG = V_DIM + LANES

TM_MLSTM = 512
TM_CONV = 512
TM_MLP = 1024
MLP_GROUP = 512
FF_CHUNK = 1024
CONV_HALO = SUBLANES
VMEM_LIMIT = 56 * 1024 * 1024

F32 = jnp.float32
BF16 = jnp.bfloat16


def _rms(x, g):
    y = x * lax.rsqrt(jnp.mean(x * x, axis=-1, keepdims=True) + EPS)
    return y * g


def _dot(a, b):
    return jnp.dot(a, b, preferred_element_type=F32)


def _scan_rows(x, op, identity):
    n = x.shape[0]
    rows = lax.broadcasted_iota(jnp.int32, x.shape, 0)
    k = 1
    while k < n:
        shifted = pltpu.roll(x, k, axis=0)
        x = op(x, jnp.where(rows >= k, shifted, identity))
        k *= 2
    return x


def _mlp_kernel(x_ref, g_ref, wup_ref, wdn_ref, o_ref):
    groups = [slice(k * MLP_GROUP, (k + 1) * MLP_GROUP) for k in range(x_ref.shape[0] // MLP_GROUP)]
    hs = [_rms(x_ref[r, :], g_ref[2:3, :]).astype(BF16) for r in groups]
    accs = []
    for h in hs:
        acc = None
        for c in range(D_FF // FF_CHUNK):
            lo, hi = c * FF_CHUNK, (c + 1) * FF_CHUNK
            u = jnp.maximum(_dot(h, wup_ref[:, lo:hi]), 0.0)
            d = _dot((u * u).astype(BF16), wdn_ref[lo:hi, :])
            acc = d if acc is None else acc + d
        accs.append(acc)
    for r, acc in zip(groups, accs):
        o_ref[r, :] = x_ref[r, :] + _rms(acc, g_ref[3:4, :])


def _mlp_layer(x2, layer, norm_g, w_up, w_down):
    n = x2.shape[0]
    this_layer = lambda i: (layer, 0, 0)
    return pl.pallas_call(
        _mlp_kernel,
        out_shape=jax.ShapeDtypeStruct(x2.shape, x2.dtype),
        grid=(n // TM_MLP,),
        in_specs=[
            pl.BlockSpec((TM_MLP, D_MODEL), lambda i: (i, 0)),
            pl.BlockSpec((None, 4, D_MODEL), this_layer),
            pl.BlockSpec((None, D_MODEL, D_FF), this_layer, pipeline_mode=pl.Buffered(1)),
            pl.BlockSpec((None, D_FF, D_MODEL), this_layer, pipeline_mode=pl.Buffered(1)),
        ],
        out_specs=pl.BlockSpec((TM_MLP, D_MODEL), lambda i: (i, 0)),
        compiler_params=pltpu.CompilerParams(
            dimension_semantics=("arbitrary",), vmem_limit_bytes=VMEM_LIMIT),
        name="mlp",
    )(x2, norm_g, w_up, w_down)


def _conv_kernel(x_ref, g_ref, wb_ref, wc_ref, wx_ref, wcv_ref, wout_ref, o_ref, u_s):
    tm = x_ref.shape[1]

    @pl.when(pl.program_id(1) == 0)
    def _():
        u_s[0:CONV_HALO, :] = jnp.zeros((CONV_HALO, D_MODEL), F32)

    x = x_ref[0]
    hn = _rms(x, g_ref[0:1, :]).astype(BF16)
    u_s[CONV_HALO:, :] = _dot(hn, wc_ref[...]) * _dot(hn, wx_ref[...])
    ua = u_s[...]
    u1 = pltpu.roll(ua, 1, axis=0)[CONV_HALO:]
    u2 = pltpu.roll(ua, 2, axis=0)[CONV_HALO:]
    conv = wcv_ref[0:1, :] * u2 + wcv_ref[1:2, :] * u1 + wcv_ref[2:3, :] * ua[CONV_HALO:]
    u_s[0:CONV_HALO, :] = ua[tm:, :]
    bg = _dot(hn, wb_ref[...])
    y = _dot((bg * conv).astype(BF16), wout_ref[...])
    o_ref[0] = x + _rms(y, g_ref[1:2, :])


def _conv_layer(x, layer, norm_g, w_in, w_conv, w_out):
    b, s, d = x.shape
    tm = TM_CONV
    j = layer // 2
    col_block = lambda k: (lambda bi, ti: (j, 0, k))
    return pl.pallas_call(
        _conv_kernel,
        out_shape=jax.ShapeDtypeStruct(x.shape, x.dtype),
        grid=(b, s // tm),
        in_specs=[
            pl.BlockSpec((1, tm, d), lambda bi, ti: (bi, ti, 0)),
            pl.BlockSpec((None, 4, d), lambda bi, ti: (layer, 0, 0)),
            pl.BlockSpec((None, d, d), col_block(0)),
            pl.BlockSpec((None, d, d), col_block(1)),
            pl.BlockSpec((None, d, d), col_block(2)),
            pl.BlockSpec((None, 3, d), col_block(0)),
            pl.BlockSpec((None, d, d), col_block(0)),
        ],
        out_specs=pl.BlockSpec((1, tm, d), lambda bi, ti: (bi, ti, 0)),
        scratch_shapes=[pltpu.VMEM((tm + CONV_HALO, d), F32)],
        compiler_params=pltpu.CompilerParams(
            dimension_semantics=("arbitrary", "arbitrary"), vmem_limit_bytes=VMEM_LIMIT),
        name="conv_mixer",
    )(x, norm_g, w_in, w_in, w_in, w_conv, w_out)


def _mlstm_kernel(x_ref, g_ref, wq_ref, wkt_ref, wv_ref, wo_ref, wg_ref, bg_ref, ghn_ref, wout_ref,
                  o_ref, v_s, h_s, c_s, m_s):
    tm = x_ref.shape[1]
    n_chunks = tm // CHUNK

    @pl.when(pl.program_id(1) == 0)
    def _():
        c_s[...] = jnp.zeros(c_s.shape, F32)
        m_s[...] = jnp.zeros(m_s.shape, F32)
        for h in range(N_HEADS):
            v_s[:, h * AUG + V_DIM:(h + 1) * AUG] = jnp.ones((tm, LANES), BF16)

    x = x_ref[0]
    hn = _rms(x, g_ref[0:1, :]).astype(BF16)

    gates = bg_ref[...]
    for c in range(n_chunks):
        gates = gates + _dot(hn[c * CHUNK:(c + 1) * CHUNK, :], wg_ref[c])
    gates = GATE_CAP * jnp.tanh(gates / GATE_CAP)
    gf = gates[:, LANES:]
    logf = -(jnp.maximum(-gf, 0.0) + jnp.log1p(jnp.exp(-jnp.abs(gf))))
    bcum = _scan_rows(logf, jnp.add, 0.0)
    g = gates[:, :LANES] - bcum
    cmax = _scan_rows(g, jnp.maximum, -jnp.inf)
    b_last = jnp.broadcast_to(bcum[CHUNK - 1:CHUNK, :], (SUBLANES, LANES))
    m_loc = b_last + jnp.broadcast_to(cmax[CHUNK - 1:CHUNK, :], (SUBLANES, LANES))

    lane = lax.broadcasted_iota(jnp.int32, (SUBLANES, LANES), 1)
    m_run = m_s[...]
    m_prev = jnp.zeros((SUBLANES, LANES), F32)
    for c in range(n_chunks):
        here = (lane >= N_HEADS * c) & (lane < N_HEADS * (c + 1))
        m_prev = jnp.where(here, m_run, m_prev)
        m_run = pltpu.roll(jnp.maximum(b_last + m_run, m_loc), N_HEADS, axis=1)
    m_s[...] = pltpu.roll(m_run, (LANES - N_HEADS * n_chunks) % LANES, axis=1)
    m_new = jnp.maximum(b_last + m_prev, m_loc)

    m_row = jnp.maximum(cmax, m_prev[0:1, :])
    decay = jnp.exp(b_last + m_prev - m_new)[0:1, :]
    inter = jnp.exp(m_prev[0:1, :] - m_row)
    floor = jnp.exp(-(bcum + m_row))
    g_t = g.T
    w_t = jnp.exp(g + (b_last - m_new)[0:1, :]).T

    q = _dot(hn, wq_ref[...])
    kt = lax.dot_general(wkt_ref[...], hn, (((1,), (1,)), ((), ())), preferred_element_type=F32)
    kt = kt * (QK_DIM ** -0.5)
    v = _dot(hn, wv_ref[...]).astype(BF16)
    for h in range(N_HEADS):
        v_s[:, h * AUG:h * AUG + V_DIM] = v[:, h * V_DIM:(h + 1) * V_DIM]

    causal = (lax.broadcasted_iota(jnp.int32, (CHUNK, CHUNK), 0)
              >= lax.broadcasted_iota(jnp.int32, (CHUNK, CHUNK), 1))
    pairs = [(c, h) for c in range(n_chunks) for h in range(N_HEADS)]
    rows_of = lambda c: slice(c * CHUNK, (c + 1) * CHUNK)
    vaug_of = lambda c, h: v_s[rows_of(c), h * AUG:(h + 1) * AUG]
    lhs, kv = {}, {}
    for c, h in pairs:
        r = N_HEADS * c + h
        qh = q[rows_of(c), h * QK_DIM:(h + 1) * QK_DIM]
        kth = kt[h * QK_DIM:(h + 1) * QK_DIM, rows_of(c)]
        e = jnp.where(causal, g_t[r:r + 1, :] - m_row[:, r:r + 1], -jnp.inf)
        scores = _dot(qh.astype(BF16), kth.astype(BF16)) * jnp.exp(e)
        lhs[c, h] = jnp.concatenate(
            [scores.astype(BF16), (qh * inter[:, r:r + 1]).astype(BF16)], axis=1)
        kv[c, h] = _dot((kth * w_t[r:r + 1, :]).astype(BF16), vaug_of(c, h))
    state = [c_s[h] for h in range(N_HEADS)]
    nd = {}
    for c, h in pairs:
        r = N_HEADS * c + h
        rhs = jnp.concatenate([vaug_of(c, h), state[h].astype(BF16)], axis=0)
        nd[c, h] = _dot(lhs[c, h], rhs)
        state[h] = decay[:, r:r + 1] * state[h] + kv[c, h]
    for h in range(N_HEADS):
        c_s[h] = state[h]
    for c, h in pairs:
        r = N_HEADS * c + h
        den = jnp.maximum(jnp.abs(nd[c, h][:, V_DIM:]), floor[:, r:r + 1])
        hh = nd[c, h][:, :V_DIM] / jnp.concatenate([den, den], axis=1)
        hh = hh * lax.rsqrt(jnp.mean(hh * hh, axis=-1, keepdims=True) + EPS)
        h_s[rows_of(c), h * V_DIM:(h + 1) * V_DIM] = hh * ghn_ref[:, h * V_DIM:(h + 1) * V_DIM]

    o = _dot(hn, wo_ref[...])
    y = _dot((jax.nn.sigmoid(o) * h_s[...]).astype(BF16), wout_ref[...])
    o_ref[0] = x + _rms(y, g_ref[1:2, :])


def _packed_gate_params(w_in, b_gates, n_chunks):
    gate_cols = w_in[:, :, 2 * QK_COLS + 2 * D_MODEL:]
    k = jnp.arange(2 * N_HEADS)
    lane_of = (k // N_HEADS) * LANES + k % N_HEADS
    lanes = lane_of[None, :] + N_HEADS * jnp.arange(n_chunks)[:, None]
    place = (lanes[:, :, None] == jnp.arange(2 * LANES)[None, None, :])
    wg = jnp.einsum("jdk,ckn->jcdn", gate_cols.astype(BF16), place.astype(BF16),
                    preferred_element_type=F32).astype(BF16)
    reps = jnp.tile(b_gates.reshape(-1, 2, 1, N_HEADS), (1, 1, n_chunks, 1)).reshape(-1, 2, N_HEADS * n_chunks)
    bg = jnp.pad(reps, ((0, 0), (0, 0), (0, LANES - N_HEADS * n_chunks))).reshape(-1, 1, 2 * LANES)
    return wg, bg


def _mlstm_layer(x, layer, norm_g, w_in, w_kt, wg, bg, g_hnorm, w_out):
    b, s, d = x.shape
    tm = TM_MLSTM
    n_chunks = tm // CHUNK
    j = layer // 2
    at = lambda *idx: (lambda bi, ti: (j,) + idx)
    return pl.pallas_call(
        _mlstm_kernel,
        out_shape=jax.ShapeDtypeStruct(x.shape, x.dtype),
        grid=(b, s // tm),
        in_specs=[
            pl.BlockSpec((1, tm, d), lambda bi, ti: (bi, ti, 0)),
            pl.BlockSpec((None, 4, d), lambda bi, ti: (layer, 0, 0)),
            pl.BlockSpec((None, d, QK_COLS), at(0, 0)),
            pl.BlockSpec((None, QK_COLS, d), at(0, 0)),
            pl.BlockSpec((None, d, d), at(0, 1)),
            pl.BlockSpec((None, d, d), at(0, 2)),
            pl.BlockSpec((None, n_chunks, d, 2 * LANES), at(0, 0, 0)),
            pl.BlockSpec((None, 1, 2 * LANES), at(0, 0)),
            pl.BlockSpec((None, 1, d), at(0, 0)),
            pl.BlockSpec((None, d, d), at(0, 0)),
        ],
        out_specs=pl.BlockSpec((1, tm, d), lambda bi, ti: (bi, ti, 0)),
        scratch_shapes=[
            pltpu.VMEM((tm, N_HEADS * AUG), BF16),
            pltpu.VMEM((tm, d), F32),
            pltpu.VMEM((N_HEADS, QK_DIM, AUG), F32),
            pltpu.VMEM((SUBLANES, LANES), F32),
        ],
        compiler_params=pltpu.CompilerParams(
            dimension_semantics=("arbitrary", "arbitrary"), vmem_limit_bytes=VMEM_LIMIT),
        name="mlstm_mixer",
    )(x, norm_g, w_in, w_kt, w_in, w_in, wg, bg, g_hnorm, w_out)


def kernel(x, norm_g, w_in_mlstm, b_gates_mlstm, g_hnorm, w_out_mlstm, w_in_conv, w_conv, w_out_conv,
           w_mlp_up, w_mlp_down):
    b, s, d = x.shape
    w_in_m = w_in_mlstm.astype(BF16)
    w_kt = jnp.swapaxes(w_in_m[:, :, QK_COLS:2 * QK_COLS], 1, 2)
    wg, bg = _packed_gate_params(w_in_mlstm, b_gates_mlstm, TM_MLSTM // CHUNK)
    ghn = g_hnorm.reshape(-1, 1, d)
    w_out_m = w_out_mlstm.astype(BF16)
    w_in_c = w_in_conv.astype(BF16)
    w_out_c = w_out_conv.astype(BF16)
    w_up = w_mlp_up.astype(BF16)
    w_down = w_mlp_down.astype(BF16)
    for i in range(DEPTH):
        if i % 2 == 0:
            x = _mlstm_layer(x, i, norm_g, w_in_m, w_kt, wg, bg, ghn, w_out_m)
        else:
            x = _conv_layer(x, i, norm_g, w_in_c, w_conv, w_out_c)
        x = _mlp_layer(x.reshape(b * s, d), i, norm_g, w_up, w_down).reshape(b, s, d)
    return x
```

```python
import jax
import jax.numpy as jnp
from jax import lax
from jax.experimental import pallas as pl
from jax.experimental.pallas import tpu as pltpu

D_MODEL = 1024
DEPTH = 4
N_HEADS = 4
QK_DIM = 128
V_DIM = 256
CHUNK = 128
GATE_CAP = 15.0
D_FF = 4 * D_MODEL
EPS = 1e-6
QK_COLS = N_HEADS * QK_DIM

LANES = 128
SUBLANES = 8

TM_MLSTM = 1024
MLSTM_GROUP = 512
GATE_ROWS = SUBLANES
TM_CONV = 512
TM_MLP = 1024
MLP_GROUP = 512
FF_CHUNK = 1024
CONV_HALO = SUBLANES
VMEM_LIMIT = 56 * 1024 * 1024

F32 = jnp.float32
BF16 = jnp.bfloat16


def _rms(x, g):
    y = x * lax.rsqrt(jnp.mean(x * x, axis=-1, keepdims=True) + EPS)
    return y * g


def _dot(a, b):
    return jnp.dot(a, b, preferred_element_type=F32)


def _scan_rows(x, op, identity):
    n = x.shape[0]
    rows = lax.broadcasted_iota(jnp.int32, x.shape, 0)
    k = 1
    while k < n:
        shifted = pltpu.roll(x, k, axis=0)
        x = op(x, jnp.where(rows >= k, shifted, identity))
        k *= 2
    return x


def _resident(block_shape, index_map):
    return pl.BlockSpec(block_shape, index_map, pipeline_mode=pl.Buffered(1))


def _mlp_kernel(x_ref, g_ref, wup_ref, wdn_ref, o_ref):
    groups = [slice(k * MLP_GROUP, (k + 1) * MLP_GROUP) for k in range(x_ref.shape[0] // MLP_GROUP)]
    hs = [_rms(x_ref[r, :], g_ref[2:3, :]).astype(BF16) for r in groups]
    accs = []
    for h in hs:
        acc = None
        for c in range(D_FF // FF_CHUNK):
            lo, hi = c * FF_CHUNK, (c + 1) * FF_CHUNK
            u = jnp.maximum(_dot(h, wup_ref[:, lo:hi]), 0.0)
            d = _dot((u * u).astype(BF16), wdn_ref[lo:hi, :])
            acc = d if acc is None else acc + d
        accs.append(acc)
    for r, acc in zip(groups, accs):
        o_ref[r, :] = x_ref[r, :] + _rms(acc, g_ref[3:4, :])


def _mlp_layer(x2, layer, norm_g, w_up, w_down):
    n = x2.shape[0]
    this_layer = lambda i: (layer, 0, 0)
    return pl.pallas_call(
        _mlp_kernel,
        out_shape=jax.ShapeDtypeStruct(x2.shape, x2.dtype),
        grid=(n // TM_MLP,),
        in_specs=[
            pl.BlockSpec((TM_MLP, D_MODEL), lambda i: (i, 0)),
            pl.BlockSpec((None, 4, D_MODEL), this_layer),
            _resident((None, D_MODEL, D_FF), this_layer),
            _resident((None, D_FF, D_MODEL), this_layer),
        ],
        out_specs=pl.BlockSpec((TM_MLP, D_MODEL), lambda i: (i, 0)),
        compiler_params=pltpu.CompilerParams(
            dimension_semantics=("arbitrary",), vmem_limit_bytes=VMEM_LIMIT),
        name="mlp",
    )(x2, norm_g, w_up, w_down)


def _conv_kernel(x_ref, g_ref, wb_ref, wc_ref, wx_ref, wcv_ref, wout_ref, o_ref, u_s):
    tm = x_ref.shape[1]

    @pl.when(pl.program_id(1) == 0)
    def _():
        u_s[0:CONV_HALO, :] = jnp.zeros((CONV_HALO, D_MODEL), F32)

    x = x_ref[0]
    hn = _rms(x, g_ref[0:1, :]).astype(BF16)
    u_s[CONV_HALO:, :] = _dot(hn, wc_ref[...]) * _dot(hn, wx_ref[...])
    ua = u_s[...]
    u1 = pltpu.roll(ua, 1, axis=0)[CONV_HALO:]
    u2 = pltpu.roll(ua, 2, axis=0)[CONV_HALO:]
    conv = wcv_ref[0:1, :] * u2 + wcv_ref[1:2, :] * u1 + wcv_ref[2:3, :] * ua[CONV_HALO:]
    u_s[0:CONV_HALO, :] = ua[tm:, :]
    bg = _dot(hn, wb_ref[...])
    y = _dot((bg * conv).astype(BF16), wout_ref[...])
    o_ref[0] = x + _rms(y, g_ref[1:2, :])


def _conv_layer(x, layer, norm_g, w_in, w_conv, w_out):
    b, s, d = x.shape
    tm = TM_CONV
    j = layer // 2
    col_block = lambda k: (lambda bi, ti: (j, 0, k))
    return pl.pallas_call(
        _conv_kernel,
        out_shape=jax.ShapeDtypeStruct(x.shape, x.dtype),
        grid=(b, s // tm),
        in_specs=[
            pl.BlockSpec((1, tm, d), lambda bi, ti: (bi, ti, 0)),
            pl.BlockSpec((None, 4, d), lambda bi, ti: (layer, 0, 0)),
            _resident((None, d, d), col_block(0)),
            _resident((None, d, d), col_block(1)),
            _resident((None, d, d), col_block(2)),
            pl.BlockSpec((None, 3, d), col_block(0)),
            _resident((None, d, d), col_block(0)),
        ],
        out_specs=pl.BlockSpec((1, tm, d), lambda bi, ti: (bi, ti, 0)),
        scratch_shapes=[pltpu.VMEM((tm + CONV_HALO, d), F32)],
        compiler_params=pltpu.CompilerParams(
            dimension_semantics=("arbitrary", "arbitrary"), vmem_limit_bytes=VMEM_LIMIT),
        name="conv_mixer",
    )(x, norm_g, w_in, w_in, w_in, w_conv, w_out)


def _mlstm_group(hn, r0, state, norm, m_run, wq_ref, wkg_ref, wv_ref, bg_ref, ghn_ref, h_s):
    n_chunks = MLSTM_GROUP // CHUNK
    lane_of = lambda c, h: GATE_ROWS * c + h

    q = _dot(hn, wq_ref[...])
    kg = lax.dot_general(wkg_ref[...], hn, (((1,), (1,)), ((), ())), preferred_element_type=F32)
    kt = kg[:QK_COLS, :] * (QK_DIM ** -0.5)
    v = _dot(hn, wv_ref[...]).astype(BF16)

    def packed(rows):
        pieces = [rows[:, c * CHUNK:(c + 1) * CHUNK] for c in range(n_chunks)]
        pieces.append(jnp.zeros((LANES - GATE_ROWS * n_chunks, CHUNK), F32))
        return jnp.concatenate(pieces, axis=0).T

    gi = packed(kg[QK_COLS:QK_COLS + GATE_ROWS, :]) + bg_ref[:, :LANES]
    gf = packed(kg[QK_COLS + GATE_ROWS:, :]) + bg_ref[:, LANES:]
    gi = GATE_CAP * jnp.tanh(gi / GATE_CAP)
    gf = GATE_CAP * jnp.tanh(gf / GATE_CAP)
    logf = -(jnp.maximum(-gf, 0.0) + jnp.log1p(jnp.exp(-jnp.abs(gf))))
    bcum = _scan_rows(logf, jnp.add, 0.0)
    g = gi - bcum
    cmax = _scan_rows(g, jnp.maximum, -jnp.inf)
    b_last = jnp.broadcast_to(bcum[CHUNK - 1:CHUNK, :], (SUBLANES, LANES))
    m_loc = b_last + jnp.broadcast_to(cmax[CHUNK - 1:CHUNK, :], (SUBLANES, LANES))

    lane = lax.broadcasted_iota(jnp.int32, (SUBLANES, LANES), 1)
    m_prev = jnp.zeros((SUBLANES, LANES), F32)
    for c in range(n_chunks):
        here = (lane >= lane_of(c, 0)) & (lane < lane_of(c, N_HEADS))
        m_prev = jnp.where(here, m_run, m_prev)
        m_run = pltpu.roll(jnp.maximum(b_last + m_run, m_loc), GATE_ROWS, axis=1)
    m_run = pltpu.roll(m_run, (LANES - GATE_ROWS * n_chunks) % LANES, axis=1)
    m_new = jnp.maximum(b_last + m_prev, m_loc)

    m_row = jnp.maximum(cmax, m_prev[0:1, :])
    decay = jnp.exp(b_last + m_prev - m_new)[0:1, :]
    inter = jnp.exp(m_prev[0:1, :] - m_row)
    floor = jnp.exp(-(bcum + m_row))
    g_t = g.T
    w_t = jnp.exp(g + (b_last - m_new)[0:1, :]).T

    causal = (lax.broadcasted_iota(jnp.int32, (CHUNK, CHUNK), 0)
              >= lax.broadcasted_iota(jnp.int32, (CHUNK, CHUNK), 1))
    pairs = [(c, h) for c in range(n_chunks) for h in range(N_HEADS)]
    rows_of = lambda c: slice(c * CHUNK, (c + 1) * CHUNK)
    qh_of = lambda c, h: q[rows_of(c), h * QK_DIM:(h + 1) * QK_DIM]
    kth_of = lambda c, h: kt[h * QK_DIM:(h + 1) * QK_DIM, rows_of(c)]
    vh_of = lambda c, h: v[rows_of(c), h * V_DIM:(h + 1) * V_DIM]

    kv, n_loc = {}, {}
    for c, h in pairs:
        kw = kth_of(c, h) * w_t[lane_of(c, h):lane_of(c, h) + 1, :]
        n_loc[c, h] = jnp.sum(kw, axis=-1, keepdims=True)
        kv[c, h] = _dot(kw.astype(BF16), vh_of(c, h))
    state, norm = list(state), list(norm)
    c_prev, n_prev = {}, {}
    for c, h in pairs:
        d = decay[:, lane_of(c, h):lane_of(c, h) + 1]
        c_prev[c, h], n_prev[c, h] = state[h], norm[h]
        state[h] = d * state[h] + kv[c, h]
        norm[h] = d * norm[h] + n_loc[c, h]

    num, den = {}, {}
    for c, h in pairs:
        r = lane_of(c, h)
        qh = qh_of(c, h)
        sq = _dot(qh.astype(BF16),
                  jnp.concatenate([kth_of(c, h).astype(BF16), n_prev[c, h].astype(BF16)], axis=1))
        e = jnp.where(causal, g_t[r:r + 1, :] - m_row[:, r:r + 1], -jnp.inf)
        scores = sq[:, :CHUNK] * jnp.exp(e)
        den[c, h] = jnp.sum(scores, axis=-1, keepdims=True) + inter[:, r:r + 1] * sq[:, CHUNK:]
        lhs = jnp.concatenate([scores.astype(BF16), (qh * inter[:, r:r + 1]).astype(BF16)], axis=1)
        rhs = jnp.concatenate([vh_of(c, h), c_prev[c, h].astype(BF16)], axis=0)
        num[c, h] = _dot(lhs, rhs)
    for c, h in pairs:
        r = lane_of(c, h)
        dd = jnp.maximum(jnp.abs(den[c, h]), floor[:, r:r + 1])
        hh = num[c, h] / jnp.concatenate([dd, dd], axis=1)
        hh = hh * lax.rsqrt(jnp.mean(hh * hh, axis=-1, keepdims=True) + EPS)
        h_s[r0 + c * CHUNK:r0 + (c + 1) * CHUNK, h * V_DIM:(h + 1) * V_DIM] = (
            hh * ghn_ref[:, h * V_DIM:(h + 1) * V_DIM])
    return state, norm, m_run


def _mlstm_kernel(x_ref, g_ref, wq_ref, wkg_ref, wv_ref, wo_ref, bg_ref, ghn_ref, wout_ref,
                  o_ref, h_s, c_s, n_s, m_s):
    tm = x_ref.shape[1]

    @pl.when(pl.program_id(1) == 0)
    def _():
        c_s[...] = jnp.zeros(c_s.shape, F32)
        n_s[...] = jnp.zeros(n_s.shape, F32)
        m_s[...] = jnp.zeros(m_s.shape, F32)

    starts = list(range(0, tm, MLSTM_GROUP))
    hns = [_rms(x_ref[0, r0:r0 + MLSTM_GROUP, :], g_ref[0:1, :]).astype(BF16) for r0 in starts]
    state = [c_s[h] for h in range(N_HEADS)]
    norm = [n_s[h] for h in range(N_HEADS)]
    m_run = m_s[...]
    for r0, hn in zip(starts, hns):
        rows = slice(r0, r0 + MLSTM_GROUP)
        state, norm, m_run = _mlstm_group(hn, r0, state, norm, m_run, wq_ref, wkg_ref, wv_ref, bg_ref,
                                          ghn_ref, h_s)
        o = _dot(hn, wo_ref[...])
        y = _dot((jax.nn.sigmoid(o) * h_s[rows, :]).astype(BF16), wout_ref[...])
        o_ref[0, rows, :] = x_ref[0, rows, :] + _rms(y, g_ref[1:2, :])
    for h in range(N_HEADS):
        c_s[h] = state[h]
        n_s[h] = norm[h]
    m_s[...] = m_run


def _k_and_gate_rows(w_in, b_gates, n_chunks):
    layers = w_in.shape[0]
    w_t = jnp.swapaxes(w_in[:, :, QK_COLS:2 * QK_COLS], 1, 2)
    gate_t = jnp.swapaxes(w_in[:, :, 2 * QK_COLS + 2 * D_MODEL:], 1, 2)
    pad = jnp.zeros((layers, GATE_ROWS - N_HEADS, D_MODEL), w_in.dtype)
    wkg = jnp.concatenate([w_t, gate_t[:, :N_HEADS], pad, gate_t[:, N_HEADS:], pad], axis=1).astype(BF16)
    per_chunk = jnp.pad(b_gates.reshape(layers, 2, 1, N_HEADS),
                        ((0, 0), (0, 0), (0, 0), (0, GATE_ROWS - N_HEADS)))
    reps = jnp.tile(per_chunk, (1, 1, n_chunks, 1)).reshape(layers, 2, GATE_ROWS * n_chunks)
    bg = jnp.pad(reps, ((0, 0), (0, 0), (0, LANES - GATE_ROWS * n_chunks))).reshape(layers, 1, 2 * LANES)
    return wkg, bg


def _mlstm_layer(x, layer, norm_g, w_in, wkg, bg, g_hnorm, w_out):
    b, s, d = x.shape
    tm = TM_MLSTM
    assert GATE_ROWS * (MLSTM_GROUP // CHUNK) <= LANES
    j = layer // 2
    at = lambda *idx: (lambda bi, ti: (j,) + idx)
    return pl.pallas_call(
        _mlstm_kernel,
        out_shape=jax.ShapeDtypeStruct(x.shape, x.dtype),
        grid=(b, s // tm),
        in_specs=[
            pl.BlockSpec((1, tm, d), lambda bi, ti: (bi, ti, 0)),
            pl.BlockSpec((None, 4, d), lambda bi, ti: (layer, 0, 0)),
            _resident((None, d, QK_COLS), at(0, 0)),
            _resident((None, QK_COLS + 2 * GATE_ROWS, d), at(0, 0)),
            _resident((None, d, d), at(0, 1)),
            _resident((None, d, d), at(0, 2)),
            pl.BlockSpec((None, 1, 2 * LANES), at(0, 0)),
            pl.BlockSpec((None, 1, d), at(0, 0)),
            _resident((None, d, d), at(0, 0)),
        ],
        out_specs=pl.BlockSpec((1, tm, d), lambda bi, ti: (bi, ti, 0)),
        scratch_shapes=[
            pltpu.VMEM((tm, d), F32),
            pltpu.VMEM((N_HEADS, QK_DIM, V_DIM), F32),
            pltpu.VMEM((N_HEADS, QK_DIM, LANES), F32),
            pltpu.VMEM((SUBLANES, LANES), F32),
        ],
        compiler_params=pltpu.CompilerParams(
            dimension_semantics=("arbitrary", "arbitrary"), vmem_limit_bytes=VMEM_LIMIT),
        name="mlstm_mixer",
    )(x, norm_g, w_in, wkg, w_in, w_in, bg, g_hnorm, w_out)


def kernel(x, norm_g, w_in_mlstm, b_gates_mlstm, g_hnorm, w_out_mlstm, w_in_conv, w_conv, w_out_conv,
           w_mlp_up, w_mlp_down):
    b, s, d = x.shape
    w_in_m = w_in_mlstm.astype(BF16)
    wkg, bg = _k_and_gate_rows(w_in_mlstm, b_gates_mlstm, MLSTM_GROUP // CHUNK)
    ghn = g_hnorm.reshape(-1, 1, d)
    w_out_m = w_out_mlstm.astype(BF16)
    w_in_c = w_in_conv.astype(BF16)
    w_out_c = w_out_conv.astype(BF16)
    w_up = w_mlp_up.astype(BF16)
    w_down = w_mlp_down.astype(BF16)
    for i in range(DEPTH):
        if i % 2 == 0:
            x = _mlstm_layer(x, i, norm_g, w_in_m, wkg, bg, ghn, w_out_m)
        else:
            x = _conv_layer(x, i, norm_g, w_in_c, w_conv, w_out_c)
        x = _mlp_layer(x.reshape(b * s, d), i, norm_g, w_up, w_down).reshape(b, s, d)
    return x
```

```python
import functools

import jax
import jax.numpy as jnp
from jax import lax
from jax.experimental import pallas as pl
from jax.experimental.pallas import tpu as pltpu

D_MODEL = 1024
DEPTH = 4
N_HEADS = 4
QK_DIM = 128
V_DIM = 256
CHUNK = 128
GATE_CAP = 15.0
D_FF = 4 * D_MODEL
EPS = 1e-6
QK_COLS = N_HEADS * QK_DIM

LANES = 128
SUBLANES = 8
BF16_ROWS = 16

TM_MLSTM = 1024
MLSTM_GROUP = 512
GATE_ROWS = SUBLANES
TM_CONV = 512
TM_MLP = 1024
MLP_GROUP = 512
FF_CHUNK = 1024
CONV_HALO = SUBLANES
VMEM_LIMIT = 56 * 1024 * 1024

F32 = jnp.float32
BF16 = jnp.bfloat16


def _rms(x, g):
    y = x * lax.rsqrt(jnp.mean(x * x, axis=-1, keepdims=True) + EPS)
    return y * g


def _dot(a, b):
    return jnp.dot(a, b, preferred_element_type=F32)


def _scan_rows(x, op, identity):
    n = x.shape[0]
    rows = lax.broadcasted_iota(jnp.int32, x.shape, 0)
    k = 1
    while k < n:
        shifted = pltpu.roll(x, k, axis=0)
        x = op(x, jnp.where(rows >= k, shifted, identity))
        k *= 2
    return x


def _resident(block_shape, index_map):
    return pl.BlockSpec(block_shape, index_map, pipeline_mode=pl.Buffered(1))


class _CastStreams:
    def __init__(self, sources, n_steps, step_of):
        self.arrays = [a for a, _ in sources]
        self.in_specs, self.out_specs, self.out_shapes = [], [], []
        for a, layer in sources:
            _, rows, cols = a.shape
            assert rows % (n_steps * BF16_ROWS) == 0
            block = rows // n_steps
            self.in_specs.append(pl.BlockSpec(
                (None, block, cols), lambda *g, layer=layer: (layer, step_of(*g), 0)))
            self.out_specs.append(pl.BlockSpec((block, cols), lambda *g: (step_of(*g), 0)))
            self.out_shapes.append(jax.ShapeDtypeStruct((rows, cols), BF16))

    def __len__(self):
        return len(self.arrays)


def _split_refs(refs, n_in, n_cast):
    ins, rest = refs[:n_in], refs[n_in:]
    cast_in, rest = rest[:n_cast], rest[n_cast:]
    out, rest = rest[0], rest[1:]
    cast_out, scratch = rest[:n_cast], rest[n_cast:]
    return ins, cast_in, out, cast_out, scratch


def _emit_casts(cast_in, cast_out):
    for src, dst in zip(cast_in, cast_out):
        dst[...] = src[...].astype(BF16)


def _mlp_kernel(n_cast, *refs):
    (x_ref, g_ref, wup_ref, wdn_ref), cast_in, o_ref, cast_out, _ = _split_refs(refs, 4, n_cast)
    _emit_casts(cast_in, cast_out)
    groups = [slice(k * MLP_GROUP, (k + 1) * MLP_GROUP) for k in range(x_ref.shape[0] // MLP_GROUP)]
    hs = [_rms(x_ref[r, :], g_ref[2:3, :]).astype(BF16) for r in groups]
    accs = []
    for h in hs:
        acc = None
        for c in range(D_FF // FF_CHUNK):
            lo, hi = c * FF_CHUNK, (c + 1) * FF_CHUNK
            u = jnp.maximum(_dot(h, wup_ref[:, lo:hi]), 0.0)
            d = _dot((u * u).astype(BF16), wdn_ref[lo:hi, :])
            acc = d if acc is None else acc + d
        accs.append(acc)
    for r, acc in zip(groups, accs):
        o_ref[r, :] = x_ref[r, :] + _rms(acc, g_ref[3:4, :])


def _mlp_layer(x2, layer, norm_g, w_up, w_down, cast_sources):
    n_steps = x2.shape[0] // TM_MLP
    casts = _CastStreams(cast_sources, n_steps, lambda i: i)
    const = lambda i: (0, 0)
    outs = pl.pallas_call(
        functools.partial(_mlp_kernel, len(casts)),
        out_shape=[jax.ShapeDtypeStruct(x2.shape, x2.dtype)] + casts.out_shapes,
        grid=(n_steps,),
        in_specs=[
            pl.BlockSpec((TM_MLP, D_MODEL), lambda i: (i, 0)),
            pl.BlockSpec((None, 4, D_MODEL), lambda i: (layer, 0, 0)),
            _resident((D_MODEL, D_FF), const),
            _resident((D_FF, D_MODEL), const),
        ] + casts.in_specs,
        out_specs=[pl.BlockSpec((TM_MLP, D_MODEL), lambda i: (i, 0))] + casts.out_specs,
        compiler_params=pltpu.CompilerParams(
            dimension_semantics=("arbitrary",), vmem_limit_bytes=VMEM_LIMIT),
        name="mlp",
    )(x2, norm_g, w_up, w_down, *casts.arrays)
    return outs[0], outs[1:]


def _conv_kernel(n_cast, *refs):
    (x_ref, g_ref, wb_ref, wc_ref, wx_ref, wcv_ref, wout_ref), cast_in, o_ref, cast_out, (u_s,) = (
        _split_refs(refs, 7, n_cast))
    tm = x_ref.shape[1]
    _emit_casts(cast_in, cast_out)

    @pl.when(pl.program_id(1) == 0)
    def _():
        u_s[0:CONV_HALO, :] = jnp.zeros((CONV_HALO, D_MODEL), F32)

    x = x_ref[0]
    hn = _rms(x, g_ref[0:1, :]).astype(BF16)
    u_s[CONV_HALO:, :] = _dot(hn, wc_ref[...]) * _dot(hn, wx_ref[...])
    ua = u_s[...]
    u1 = pltpu.roll(ua, 1, axis=0)[CONV_HALO:]
    u2 = pltpu.roll(ua, 2, axis=0)[CONV_HALO:]
    conv = wcv_ref[0:1, :] * u2 + wcv_ref[1:2, :] * u1 + wcv_ref[2:3, :] * ua[CONV_HALO:]
    u_s[0:CONV_HALO, :] = ua[tm:, :]
    bg = _dot(hn, wb_ref[...])
    y = _dot((bg * conv).astype(BF16), wout_ref[...])
    o_ref[0] = x + _rms(y, g_ref[1:2, :])


def _conv_layer(x, layer, norm_g, w_in, w_conv, w_out, cast_sources):
    b, s, d = x.shape
    tm = TM_CONV
    n_t = s // tm
    casts = _CastStreams(cast_sources, b * n_t, lambda bi, ti: bi * n_t + ti)
    col_block = lambda k: (lambda bi, ti: (0, k))
    outs = pl.pallas_call(
        functools.partial(_conv_kernel, len(casts)),
        out_shape=[jax.ShapeDtypeStruct(x.shape, x.dtype)] + casts.out_shapes,
        grid=(b, n_t),
        in_specs=[
            pl.BlockSpec((1, tm, d), lambda bi, ti: (bi, ti, 0)),
            pl.BlockSpec((None, 4, d), lambda bi, ti: (layer, 0, 0)),
            _resident((d, d), col_block(0)),
            _resident((d, d), col_block(1)),
            _resident((d, d), col_block(2)),
            pl.BlockSpec((None, 3, d), lambda bi, ti: (layer // 2, 0, 0)),
            _resident((d, d), col_block(0)),
        ] + casts.in_specs,
        out_specs=[pl.BlockSpec((1, tm, d), lambda bi, ti: (bi, ti, 0))] + casts.out_specs,
        scratch_shapes=[pltpu.VMEM((tm + CONV_HALO, d), F32)],
        compiler_params=pltpu.CompilerParams(
            dimension_semantics=("arbitrary", "arbitrary"), vmem_limit_bytes=VMEM_LIMIT),
        name="conv_mixer",
    )(x, norm_g, w_in, w_in, w_in, w_conv, w_out, *casts.arrays)
    return outs[0], outs[1:]


def _mlstm_group(hn, r0, state, norm, m_run, wq_ref, wkg_ref, wv_ref, bg_ref, ghn_ref, h_s):
    n_chunks = MLSTM_GROUP // CHUNK
    lane_of = lambda c, h: GATE_ROWS * c + h

    q = _dot(hn, wq_ref[...])
    kg = lax.dot_general(wkg_ref[...], hn, (((1,), (1,)), ((), ())), preferred_element_type=F32)
    kt = kg[:QK_COLS, :] * (QK_DIM ** -0.5)
    v = _dot(hn, wv_ref[...]).astype(BF16)

    def packed(rows):
        pieces = [rows[:, c * CHUNK:(c + 1) * CHUNK] for c in range(n_chunks)]
        pieces.append(jnp.zeros((LANES - GATE_ROWS * n_chunks, CHUNK), F32))
        return jnp.concatenate(pieces, axis=0).T

    gi = packed(kg[QK_COLS:QK_COLS + GATE_ROWS, :]) + bg_ref[:, :LANES]
    gf = packed(kg[QK_COLS + GATE_ROWS:, :]) + bg_ref[:, LANES:]
    gi = GATE_CAP * jnp.tanh(gi / GATE_CAP)
    gf = GATE_CAP * jnp.tanh(gf / GATE_CAP)
    logf = -(jnp.maximum(-gf, 0.0) + jnp.log1p(jnp.exp(-jnp.abs(gf))))
    bcum = _scan_rows(logf, jnp.add, 0.0)
    g = gi - bcum
    cmax = _scan_rows(g, jnp.maximum, -jnp.inf)
    b_last = jnp.broadcast_to(bcum[CHUNK - 1:CHUNK, :], (SUBLANES, LANES))
    m_loc = b_last + jnp.broadcast_to(cmax[CHUNK - 1:CHUNK, :], (SUBLANES, LANES))

    lane = lax.broadcasted_iota(jnp.int32, (SUBLANES, LANES), 1)
    m_prev = jnp.zeros((SUBLANES, LANES), F32)
    for c in range(n_chunks):
        here = (lane >= lane_of(c, 0)) & (lane < lane_of(c, N_HEADS))
        m_prev = jnp.where(here, m_run, m_prev)
        m_run = pltpu.roll(jnp.maximum(b_last + m_run, m_loc), GATE_ROWS, axis=1)
    m_run = pltpu.roll(m_run, (LANES - GATE_ROWS * n_chunks) % LANES, axis=1)
    m_new = jnp.maximum(b_last + m_prev, m_loc)

    m_row = jnp.maximum(cmax, m_prev[0:1, :])
    decay = jnp.exp(b_last + m_prev - m_new)[0:1, :]
    inter = jnp.exp(m_prev[0:1, :] - m_row)
    floor = jnp.exp(-(bcum + m_row))
    g_t = g.T
    w_t = jnp.exp(g + (b_last - m_new)[0:1, :]).T

    causal = (lax.broadcasted_iota(jnp.int32, (CHUNK, CHUNK), 0)
              >= lax.broadcasted_iota(jnp.int32, (CHUNK, CHUNK), 1))
    pairs = [(c, h) for c in range(n_chunks) for h in range(N_HEADS)]
    rows_of = lambda c: slice(c * CHUNK, (c + 1) * CHUNK)
    qh_of = lambda c, h: q[rows_of(c), h * QK_DIM:(h + 1) * QK_DIM]
    kth_of = lambda c, h: kt[h * QK_DIM:(h + 1) * QK_DIM, rows_of(c)]
    vh_of = lambda c, h: v[rows_of(c), h * V_DIM:(h + 1) * V_DIM]

    kv, n_loc = {}, {}
    for c, h in pairs:
        kw = kth_of(c, h) * w_t[lane_of(c, h):lane_of(c, h) + 1, :]
        n_loc[c, h] = jnp.sum(kw, axis=-1, keepdims=True)
        kv[c, h] = _dot(kw.astype(BF16), vh_of(c, h))
    state, norm = list(state), list(norm)
    c_prev, n_prev = {}, {}
    for c, h in pairs:
        d = decay[:, lane_of(c, h):lane_of(c, h) + 1]
        c_prev[c, h], n_prev[c, h] = state[h], norm[h]
        state[h] = d * state[h] + kv[c, h]
        norm[h] = d * norm[h] + n_loc[c, h]

    num, den = {}, {}
    for c, h in pairs:
        r = lane_of(c, h)
        qh = qh_of(c, h)
        sq = _dot(qh.astype(BF16),
                  jnp.concatenate([kth_of(c, h).astype(BF16), n_prev[c, h].astype(BF16)], axis=1))
        e = jnp.where(causal, g_t[r:r + 1, :] - m_row[:, r:r + 1], -jnp.inf)
        scores = sq[:, :CHUNK] * jnp.exp(e)
        den[c, h] = jnp.sum(scores, axis=-1, keepdims=True) + inter[:, r:r + 1] * sq[:, CHUNK:]
        lhs = jnp.concatenate([scores.astype(BF16), (qh * inter[:, r:r + 1]).astype(BF16)], axis=1)
        rhs = jnp.concatenate([vh_of(c, h), c_prev[c, h].astype(BF16)], axis=0)
        num[c, h] = _dot(lhs, rhs)
    for c, h in pairs:
        r = lane_of(c, h)
        dd = jnp.maximum(jnp.abs(den[c, h]), floor[:, r:r + 1])
        hh = num[c, h] / jnp.concatenate([dd, dd], axis=1)
        hh = hh * lax.rsqrt(jnp.mean(hh * hh, axis=-1, keepdims=True) + EPS)
        h_s[r0 + c * CHUNK:r0 + (c + 1) * CHUNK, h * V_DIM:(h + 1) * V_DIM] = (
            hh * ghn_ref[:, h * V_DIM:(h + 1) * V_DIM])
    return state, norm, m_run


def _mlstm_kernel(n_cast, *refs):
    ((x_ref, g_ref, wq_ref, wkg_ref, wv_ref, wo_ref, bg_ref, ghn_ref, wout_ref), cast_in, o_ref, cast_out,
     (h_s, c_s, n_s, m_s)) = _split_refs(refs, 9, n_cast)
    tm = x_ref.shape[1]
    _emit_casts(cast_in, cast_out)

    @pl.when(pl.program_id(1) == 0)
    def _():
        c_s[...] = jnp.zeros(c_s.shape, F32)
        n_s[...] = jnp.zeros(n_s.shape, F32)
        m_s[...] = jnp.zeros(m_s.shape, F32)

    starts = list(range(0, tm, MLSTM_GROUP))
    hns = [_rms(x_ref[0, r0:r0 + MLSTM_GROUP, :], g_ref[0:1, :]).astype(BF16) for r0 in starts]
    state = [c_s[h] for h in range(N_HEADS)]
    norm = [n_s[h] for h in range(N_HEADS)]
    m_run = m_s[...]
    for r0, hn in zip(starts, hns):
        rows = slice(r0, r0 + MLSTM_GROUP)
        state, norm, m_run = _mlstm_group(hn, r0, state, norm, m_run, wq_ref, wkg_ref, wv_ref, bg_ref,
                                          ghn_ref, h_s)
        o = _dot(hn, wo_ref[...])
        y = _dot((jax.nn.sigmoid(o) * h_s[rows, :]).astype(BF16), wout_ref[...])
        o_ref[0, rows, :] = x_ref[0, rows, :] + _rms(y, g_ref[1:2, :])
    for h in range(N_HEADS):
        c_s[h] = state[h]
        n_s[h] = norm[h]
    m_s[...] = m_run


def _k_and_gate_rows(w_in, b_gates, n_chunks):
    layers = w_in.shape[0]
    w_t = jnp.swapaxes(w_in[:, :, QK_COLS:2 * QK_COLS], 1, 2)
    gate_t = jnp.swapaxes(w_in[:, :, 2 * QK_COLS + 2 * D_MODEL:], 1, 2)
    pad = jnp.zeros((layers, GATE_ROWS - N_HEADS, D_MODEL), w_in.dtype)
    wkg = jnp.concatenate([w_t, gate_t[:, :N_HEADS], pad, gate_t[:, N_HEADS:], pad], axis=1).astype(BF16)
    per_chunk = jnp.pad(b_gates.reshape(layers, 2, 1, N_HEADS),
                        ((0, 0), (0, 0), (0, 0), (0, GATE_ROWS - N_HEADS)))
    reps = jnp.tile(per_chunk, (1, 1, n_chunks, 1)).reshape(layers, 2, GATE_ROWS * n_chunks)
    bg = jnp.pad(reps, ((0, 0), (0, 0), (0, LANES - GATE_ROWS * n_chunks))).reshape(layers, 1, 2 * LANES)
    return wkg, bg


def _mlstm_layer(x, layer, norm_g, w_in, wkg, bg, g_hnorm, w_out, cast_sources):
    b, s, d = x.shape
    tm = TM_MLSTM
    n_t = s // tm
    assert GATE_ROWS * (MLSTM_GROUP // CHUNK) <= LANES
    j = layer // 2
    casts = _CastStreams(cast_sources, b * n_t, lambda bi, ti: bi * n_t + ti)
    col_block = lambda k: (lambda bi, ti: (0, k))
    this_layer = lambda bi, ti: (j, 0, 0)
    outs = pl.pallas_call(
        functools.partial(_mlstm_kernel, len(casts)),
        out_shape=[jax.ShapeDtypeStruct(x.shape, x.dtype)] + casts.out_shapes,
        grid=(b, n_t),
        in_specs=[
            pl.BlockSpec((1, tm, d), lambda bi, ti: (bi, ti, 0)),
            pl.BlockSpec((None, 4, d), lambda bi, ti: (layer, 0, 0)),
            _resident((d, QK_COLS), col_block(0)),
            _resident((None, QK_COLS + 2 * GATE_ROWS, d), this_layer),
            _resident((d, d), col_block(1)),
            _resident((d, d), col_block(2)),
            pl.BlockSpec((None, 1, 2 * LANES), this_layer),
            pl.BlockSpec((None, 1, d), this_layer),
            _resident((d, d), col_block(0)),
        ] + casts.in_specs,
        out_specs=[pl.BlockSpec((1, tm, d), lambda bi, ti: (bi, ti, 0))] + casts.out_specs,
        scratch_shapes=[
            pltpu.VMEM((tm, d), F32),
            pltpu.VMEM((N_HEADS, QK_DIM, V_DIM), F32),
            pltpu.VMEM((N_HEADS, QK_DIM, LANES), F32),
            pltpu.VMEM((SUBLANES, LANES), F32),
        ],
        compiler_params=pltpu.CompilerParams(
            dimension_semantics=("arbitrary", "arbitrary"), vmem_limit_bytes=VMEM_LIMIT),
        name="mlstm_mixer",
    )(x, norm_g, w_in, wkg, w_in, w_in, bg, g_hnorm, w_out, *casts.arrays)
    return outs[0], outs[1:]


def kernel(x, norm_g, w_in_mlstm, b_gates_mlstm, g_hnorm, w_out_mlstm, w_in_conv, w_conv, w_out_conv,
           w_mlp_up, w_mlp_down):
    b, s, d = x.shape
    wkg, bg = _k_and_gate_rows(w_in_mlstm, b_gates_mlstm, MLSTM_GROUP // CHUNK)
    ghn = g_hnorm.reshape(-1, 1, d)
    mixer_w = (w_in_mlstm[0].astype(BF16), w_out_mlstm[0].astype(BF16))
    for i in range(DEPTH):
        j = i // 2
        mlp_sources = [(w_mlp_up, i), (w_mlp_down, i)]
        if i % 2 == 0:
            x, mlp_w = _mlstm_layer(x, i, norm_g, mixer_w[0], wkg, bg, ghn, mixer_w[1], mlp_sources)
        else:
            x, mlp_w = _conv_layer(x, i, norm_g, mixer_w[0], w_conv, mixer_w[1], mlp_sources)
        if i + 1 == DEPTH:
            next_sources = []
        elif (i + 1) % 2 == 0:
            next_sources = [(w_in_mlstm, (i + 1) // 2), (w_out_mlstm, (i + 1) // 2)]
        else:
            next_sources = [(w_in_conv, (i + 1) // 2), (w_out_conv, (i + 1) // 2)]
        x2, mixer_w = _mlp_layer(x.reshape(b * s, d), i, norm_g, mlp_w[0], mlp_w[1], next_sources)
        x = x2.reshape(b, s, d)
    return x
```

```python
import functools

import jax
import jax.numpy as jnp
from jax import lax
from jax.experimental import pallas as pl
from jax.experimental.pallas import tpu as pltpu

D_MODEL = 1024
DEPTH = 4
N_HEADS = 4
QK_DIM = 128
V_DIM = 256
CHUNK = 128
GATE_CAP = 15.0
D_FF = 4 * D_MODEL
EPS = 1e-6
QK_COLS = N_HEADS * QK_DIM

LANES = 128
SUBLANES = 8
BF16_ROWS = 16

TM_MLSTM = 1024
MLSTM_GROUP = 512
GATE_ROWS = SUBLANES
TM_CONV = 512
TM_MLP = 1024
MLP_GROUP = 512
FF_CHUNK = 1024
CONV_HALO = SUBLANES
VMEM_LIMIT = 56 * 1024 * 1024

F32 = jnp.float32
BF16 = jnp.bfloat16


def _rms(x, g):
    y = x * lax.rsqrt(jnp.mean(x * x, axis=-1, keepdims=True) + EPS)
    return y * g


def _dot(a, b):
    return jnp.dot(a, b, preferred_element_type=F32)


def _scan_rows(x, op, identity):
    n = x.shape[0]
    rows = lax.broadcasted_iota(jnp.int32, x.shape, 0)
    k = 1
    while k < n:
        shifted = pltpu.roll(x, k, axis=0)
        x = op(x, jnp.where(rows >= k, shifted, identity))
        k *= 2
    return x


def _resident(block_shape, index_map):
    return pl.BlockSpec(block_shape, index_map, pipeline_mode=pl.Buffered(1))


class _CastStreams:
    def __init__(self, sources, n_steps, step_of):
        self.arrays = [a for a, _ in sources]
        self.in_specs, self.out_specs, self.out_shapes = [], [], []
        for a, layer in sources:
            _, rows, cols = a.shape
            assert rows % (n_steps * BF16_ROWS) == 0
            block = rows // n_steps
            self.in_specs.append(pl.BlockSpec(
                (None, block, cols), lambda *g, layer=layer: (layer, step_of(*g), 0)))
            self.out_specs.append(pl.BlockSpec((block, cols), lambda *g: (step_of(*g), 0)))
            self.out_shapes.append(jax.ShapeDtypeStruct((rows, cols), BF16))

    def __len__(self):
        return len(self.arrays)


def _split_refs(refs, n_in, n_cast):
    ins, rest = refs[:n_in], refs[n_in:]
    cast_in, rest = rest[:n_cast], rest[n_cast:]
    out, rest = rest[0], rest[1:]
    cast_out, scratch = rest[:n_cast], rest[n_cast:]
    return ins, cast_in, out, cast_out, scratch


def _emit_casts(cast_in, cast_out):
    for src, dst in zip(cast_in, cast_out):
        dst[...] = src[...].astype(BF16)


def _mlp_kernel(n_cast, *refs):
    (x_ref, g_ref, wup_ref, wdn_ref), cast_in, o_ref, cast_out, _ = _split_refs(refs, 4, n_cast)
    _emit_casts(cast_in, cast_out)
    groups = [slice(k * MLP_GROUP, (k + 1) * MLP_GROUP) for k in range(x_ref.shape[0] // MLP_GROUP)]
    hs = [_rms(x_ref[r, :], g_ref[2:3, :]).astype(BF16) for r in groups]
    accs = []
    for h in hs:
        acc = None
        for c in range(D_FF // FF_CHUNK):
            lo, hi = c * FF_CHUNK, (c + 1) * FF_CHUNK
            u = jnp.maximum(_dot(h, wup_ref[:, lo:hi]), 0.0)
            d = _dot((u * u).astype(BF16), wdn_ref[lo:hi, :])
            acc = d if acc is None else acc + d
        accs.append(acc)
    for r, acc in zip(groups, accs):
        o_ref[r, :] = x_ref[r, :] + _rms(acc, g_ref[3:4, :])


def _mlp_layer(x2, layer, norm_g, w_up, w_down, cast_sources):
    n_steps = x2.shape[0] // TM_MLP
    casts = _CastStreams(cast_sources, n_steps, lambda i: i)
    const = lambda i: (0, 0)
    outs = pl.pallas_call(
        functools.partial(_mlp_kernel, len(casts)),
        out_shape=[jax.ShapeDtypeStruct(x2.shape, x2.dtype)] + casts.out_shapes,
        grid=(n_steps,),
        in_specs=[
            pl.BlockSpec((TM_MLP, D_MODEL), lambda i: (i, 0)),
            pl.BlockSpec((None, 4, D_MODEL), lambda i: (layer, 0, 0)),
            _resident((D_MODEL, D_FF), const),
            _resident((D_FF, D_MODEL), const),
        ] + casts.in_specs,
        out_specs=[pl.BlockSpec((TM_MLP, D_MODEL), lambda i: (i, 0))] + casts.out_specs,
        compiler_params=pltpu.CompilerParams(
            dimension_semantics=("arbitrary",), vmem_limit_bytes=VMEM_LIMIT),
        name="mlp",
    )(x2, norm_g, w_up, w_down, *casts.arrays)
    return outs[0], outs[1:]


def _conv_kernel(n_cast, *refs):
    (x_ref, g_ref, wb_ref, wc_ref, wx_ref, wcv_ref, wout_ref), cast_in, o_ref, cast_out, (u_s,) = (
        _split_refs(refs, 7, n_cast))
    tm = x_ref.shape[1]
    _emit_casts(cast_in, cast_out)

    @pl.when(pl.program_id(1) == 0)
    def _():
        u_s[0:CONV_HALO, :] = jnp.zeros((CONV_HALO, D_MODEL), F32)

    x = x_ref[0]
    hn = _rms(x, g_ref[0:1, :]).astype(BF16)
    u_s[CONV_HALO:, :] = _dot(hn, wc_ref[...]) * _dot(hn, wx_ref[...])
    ua = u_s[...]
    u1 = pltpu.roll(ua, 1, axis=0)[CONV_HALO:]
    u2 = pltpu.roll(ua, 2, axis=0)[CONV_HALO:]
    conv = wcv_ref[0:1, :] * u2 + wcv_ref[1:2, :] * u1 + wcv_ref[2:3, :] * ua[CONV_HALO:]
    u_s[0:CONV_HALO, :] = ua[tm:, :]
    bg = _dot(hn, wb_ref[...])
    y = _dot((bg * conv).astype(BF16), wout_ref[...])
    o_ref[0] = x + _rms(y, g_ref[1:2, :])


def _conv_layer(x, layer, norm_g, w_in, w_conv, w_out, cast_sources):
    b, s, d = x.shape
    tm = TM_CONV
    n_t = s // tm
    casts = _CastStreams(cast_sources, b * n_t, lambda bi, ti: bi * n_t + ti)
    col_block = lambda k: (lambda bi, ti: (0, k))
    outs = pl.pallas_call(
        functools.partial(_conv_kernel, len(casts)),
        out_shape=[jax.ShapeDtypeStruct(x.shape, x.dtype)] + casts.out_shapes,
        grid=(b, n_t),
        in_specs=[
            pl.BlockSpec((1, tm, d), lambda bi, ti: (bi, ti, 0)),
            pl.BlockSpec((None, 4, d), lambda bi, ti: (layer, 0, 0)),
            _resident((d, d), col_block(0)),
            _resident((d, d), col_block(1)),
            _resident((d, d), col_block(2)),
            pl.BlockSpec((None, 3, d), lambda bi, ti: (layer // 2, 0, 0)),
            _resident((d, d), col_block(0)),
        ] + casts.in_specs,
        out_specs=[pl.BlockSpec((1, tm, d), lambda bi, ti: (bi, ti, 0))] + casts.out_specs,
        scratch_shapes=[pltpu.VMEM((tm + CONV_HALO, d), F32)],
        compiler_params=pltpu.CompilerParams(
            dimension_semantics=("arbitrary", "arbitrary"), vmem_limit_bytes=VMEM_LIMIT),
        name="conv_mixer",
    )(x, norm_g, w_in, w_in, w_in, w_conv, w_out, *casts.arrays)
    return outs[0], outs[1:]


def _mlstm_group(hn, r0, state, norm, m_run, wq_ref, wkg_ref, wv_ref, bg_ref, ghn_ref, h_s):
    n_chunks = MLSTM_GROUP // CHUNK
    lane_of = lambda c, h: GATE_ROWS * c + h

    q = _dot(hn, wq_ref[...])
    kg = lax.dot_general(wkg_ref[...], hn, (((1,), (1,)), ((), ())), preferred_element_type=F32)
    kt = kg[:QK_COLS, :] * (QK_DIM ** -0.5)
    v = _dot(hn, wv_ref[...]).astype(BF16)

    def packed(rows):
        pieces = [rows[:, c * CHUNK:(c + 1) * CHUNK] for c in range(n_chunks)]
        pieces.append(jnp.zeros((LANES - GATE_ROWS * n_chunks, CHUNK), F32))
        return jnp.concatenate(pieces, axis=0).T

    gi = packed(kg[QK_COLS:QK_COLS + GATE_ROWS, :]) + bg_ref[:, :LANES]
    gf = packed(kg[QK_COLS + GATE_ROWS:, :]) + bg_ref[:, LANES:]
    gi = GATE_CAP * jnp.tanh(gi / GATE_CAP)
    gf = GATE_CAP * jnp.tanh(gf / GATE_CAP)
    logf = -(jnp.maximum(-gf, 0.0) + jnp.log1p(jnp.exp(-jnp.abs(gf))))
    bcum = _scan_rows(logf, jnp.add, 0.0)
    g = gi - bcum
    cmax = _scan_rows(g, jnp.maximum, -jnp.inf)
    b_last = jnp.broadcast_to(bcum[CHUNK - 1:CHUNK, :], (SUBLANES, LANES))
    m_loc = b_last + jnp.broadcast_to(cmax[CHUNK - 1:CHUNK, :], (SUBLANES, LANES))

    lane = lax.broadcasted_iota(jnp.int32, (SUBLANES, LANES), 1)
    m_prev = jnp.zeros((SUBLANES, LANES), F32)
    for c in range(n_chunks):
        here = (lane >= lane_of(c, 0)) & (lane < lane_of(c, N_HEADS))
        m_prev = jnp.where(here, m_run, m_prev)
        m_run = pltpu.roll(jnp.maximum(b_last + m_run, m_loc), GATE_ROWS, axis=1)
    m_run = pltpu.roll(m_run, (LANES - GATE_ROWS * n_chunks) % LANES, axis=1)
    m_new = jnp.maximum(b_last + m_prev, m_loc)

    m_row = jnp.maximum(cmax, m_prev[0:1, :])
    decay = jnp.exp(b_last + m_prev - m_new)[0:1, :]
    inter = jnp.exp(m_prev[0:1, :] - m_row)
    floor = jnp.exp(-(bcum + m_row))
    g_t = g.T
    w_t = jnp.exp(g + (b_last - m_new)[0:1, :]).T

    causal = (lax.broadcasted_iota(jnp.int32, (CHUNK, CHUNK), 0)
              >= lax.broadcasted_iota(jnp.int32, (CHUNK, CHUNK), 1))
    pairs = [(c, h) for c in range(n_chunks) for h in range(N_HEADS)]
    rows_of = lambda c: slice(c * CHUNK, (c + 1) * CHUNK)
    qh_of = lambda c, h: q[rows_of(c), h * QK_DIM:(h + 1) * QK_DIM]
    kth_of = lambda c, h: kt[h * QK_DIM:(h + 1) * QK_DIM, rows_of(c)]
    vh_of = lambda c, h: v[rows_of(c), h * V_DIM:(h + 1) * V_DIM]

    kv, n_loc = {}, {}
    for c, h in pairs:
        kw = kth_of(c, h) * w_t[lane_of(c, h):lane_of(c, h) + 1, :]
        n_loc[c, h] = jnp.sum(kw, axis=-1, keepdims=True)
        kv[c, h] = _dot(kw.astype(BF16), vh_of(c, h))
    state, norm = list(state), list(norm)
    c_prev, n_prev = {}, {}
    for c, h in pairs:
        d = decay[:, lane_of(c, h):lane_of(c, h) + 1]
        c_prev[c, h], n_prev[c, h] = state[h], norm[h]
        state[h] = d * state[h] + kv[c, h]
        norm[h] = d * norm[h] + n_loc[c, h]

    num, den = {}, {}
    for c, h in pairs:
        r = lane_of(c, h)
        qh = qh_of(c, h)
        sq = _dot(qh.astype(BF16),
                  jnp.concatenate([kth_of(c, h).astype(BF16), n_prev[c, h].astype(BF16)], axis=1))
        e = jnp.where(causal, g_t[r:r + 1, :] - m_row[:, r:r + 1], -jnp.inf)
        scores = sq[:, :CHUNK] * jnp.exp(e)
        den[c, h] = jnp.sum(scores, axis=-1, keepdims=True) + inter[:, r:r + 1] * sq[:, CHUNK:]
        lhs = jnp.concatenate([scores.astype(BF16), (qh * inter[:, r:r + 1]).astype(BF16)], axis=1)
        rhs = jnp.concatenate([vh_of(c, h), c_prev[c, h].astype(BF16)], axis=0)
        num[c, h] = _dot(lhs, rhs)
    for c, h in pairs:
        r = lane_of(c, h)
        dd = jnp.maximum(jnp.abs(den[c, h]), floor[:, r:r + 1])
        hh = num[c, h] / jnp.concatenate([dd, dd], axis=1)
        hh = hh * lax.rsqrt(jnp.mean(hh * hh, axis=-1, keepdims=True) + EPS)
        h_s[r0 + c * CHUNK:r0 + (c + 1) * CHUNK, h * V_DIM:(h + 1) * V_DIM] = (
            hh * ghn_ref[:, h * V_DIM:(h + 1) * V_DIM])
    return state, norm, m_run


def _mlstm_kernel(n_cast, *refs):
    ((x_ref, g_ref, wq_ref, wkg_ref, wv_ref, wo_ref, bg_ref, ghn_ref, wout_ref), cast_in, o_ref, cast_out,
     (h_s, c_s, n_s, m_s)) = _split_refs(refs, 9, n_cast)
    tm = x_ref.shape[1]
    _emit_casts(cast_in, cast_out)

    @pl.when(pl.program_id(1) == 0)
    def _():
        c_s[...] = jnp.zeros(c_s.shape, F32)
        n_s[...] = jnp.zeros(n_s.shape, F32)
        m_s[...] = jnp.zeros(m_s.shape, F32)

    starts = list(range(0, tm, MLSTM_GROUP))
    hns = [_rms(x_ref[0, r0:r0 + MLSTM_GROUP, :], g_ref[0:1, :]).astype(BF16) for r0 in starts]
    state = [c_s[h] for h in range(N_HEADS)]
    norm = [n_s[h] for h in range(N_HEADS)]
    m_run = m_s[...]
    for r0, hn in zip(starts, hns):
        rows = slice(r0, r0 + MLSTM_GROUP)
        state, norm, m_run = _mlstm_group(hn, r0, state, norm, m_run, wq_ref, wkg_ref, wv_ref, bg_ref,
                                          ghn_ref, h_s)
        o = _dot(hn, wo_ref[...])
        y = _dot((jax.nn.sigmoid(o) * h_s[rows, :]).astype(BF16), wout_ref[...])
        o_ref[0, rows, :] = x_ref[0, rows, :] + _rms(y, g_ref[1:2, :])
    for h in range(N_HEADS):
        c_s[h] = state[h]
        n_s[h] = norm[h]
    m_s[...] = m_run


def _k_and_gate_rows(w_in, b_gates, n_chunks):
    layers = w_in.shape[0]
    w_t = jnp.swapaxes(w_in[:, :, QK_COLS:2 * QK_COLS], 1, 2)
    gate_t = jnp.swapaxes(w_in[:, :, 2 * QK_COLS + 2 * D_MODEL:], 1, 2)
    pad = jnp.zeros((layers, GATE_ROWS - N_HEADS, D_MODEL), w_in.dtype)
    wkg = jnp.concatenate([w_t, gate_t[:, :N_HEADS], pad, gate_t[:, N_HEADS:], pad], axis=1).astype(BF16)
    per_chunk = jnp.pad(b_gates.reshape(layers, 2, 1, N_HEADS),
                        ((0, 0), (0, 0), (0, 0), (0, GATE_ROWS - N_HEADS)))
    reps = jnp.tile(per_chunk, (1, 1, n_chunks, 1)).reshape(layers, 2, GATE_ROWS * n_chunks)
    bg = jnp.pad(reps, ((0, 0), (0, 0), (0, LANES - GATE_ROWS * n_chunks))).reshape(layers, 1, 2 * LANES)
    return wkg, bg


def _mlstm_layer(x, layer, norm_g, w_in, wkg, bg, g_hnorm, w_out, cast_sources):
    b, s, d = x.shape
    tm = TM_MLSTM
    n_t = s // tm
    assert GATE_ROWS * (MLSTM_GROUP // CHUNK) <= LANES
    j = layer // 2
    casts = _CastStreams(cast_sources, b * n_t, lambda bi, ti: bi * n_t + ti)
    col_block = lambda k: (lambda bi, ti: (j, 0, k))
    this_layer = lambda bi, ti: (j, 0, 0)
    outs = pl.pallas_call(
        functools.partial(_mlstm_kernel, len(casts)),
        out_shape=[jax.ShapeDtypeStruct(x.shape, x.dtype)] + casts.out_shapes,
        grid=(b, n_t),
        in_specs=[
            pl.BlockSpec((1, tm, d), lambda bi, ti: (bi, ti, 0)),
            pl.BlockSpec((None, 4, d), lambda bi, ti: (layer, 0, 0)),
            _resident((None, d, QK_COLS), col_block(0)),
            _resident((None, QK_COLS + 2 * GATE_ROWS, d), this_layer),
            _resident((None, d, d), col_block(1)),
            _resident((None, d, d), col_block(2)),
            pl.BlockSpec((None, 1, 2 * LANES), this_layer),
            pl.BlockSpec((None, 1, d), this_layer),
            _resident((None, d, d), this_layer),
        ] + casts.in_specs,
        out_specs=[pl.BlockSpec((1, tm, d), lambda bi, ti: (bi, ti, 0))] + casts.out_specs,
        scratch_shapes=[
            pltpu.VMEM((tm, d), F32),
            pltpu.VMEM((N_HEADS, QK_DIM, V_DIM), F32),
            pltpu.VMEM((N_HEADS, QK_DIM, LANES), F32),
            pltpu.VMEM((SUBLANES, LANES), F32),
        ],
        compiler_params=pltpu.CompilerParams(
            dimension_semantics=("arbitrary", "arbitrary"), vmem_limit_bytes=VMEM_LIMIT),
        name="mlstm_mixer",
    )(x, norm_g, w_in, wkg, w_in, w_in, bg, g_hnorm, w_out, *casts.arrays)
    return outs[0], outs[1:]


def kernel(x, norm_g, w_in_mlstm, b_gates_mlstm, g_hnorm, w_out_mlstm, w_in_conv, w_conv, w_out_conv,
           w_mlp_up, w_mlp_down):
    b, s, d = x.shape
    wkg, bg = _k_and_gate_rows(w_in_mlstm, b_gates_mlstm, MLSTM_GROUP // CHUNK)
    w_qkvo = w_in_mlstm[:, :, :2 * QK_COLS + 2 * d].astype(BF16)
    w_out_m = w_out_mlstm.astype(BF16)
    ghn = g_hnorm.reshape(-1, 1, d)
    conv_w = None
    for i in range(DEPTH):
        mlp_sources = [(w_mlp_up, i), (w_mlp_down, i)]
        if i % 2 == 0:
            x, mlp_w = _mlstm_layer(x, i, norm_g, w_qkvo, wkg, bg, ghn, w_out_m, mlp_sources)
            next_sources = [(w_in_conv, i // 2), (w_out_conv, i // 2)]
        else:
            x, mlp_w = _conv_layer(x, i, norm_g, conv_w[0], w_conv, conv_w[1], mlp_sources)
            next_sources = []
        x2, conv_w = _mlp_layer(x.reshape(b * s, d), i, norm_g, mlp_w[0], mlp_w[1], next_sources)
        x = x2.reshape(b, s, d)
    return x
```
